```python
import math
import jax
import jax.numpy as jnp
from jax import lax
import numpy as np

D_MODEL = 1024
BATCH = 4
SEQ = 8192
DEPTH = 2

CHUNK = 64
N_META = 16
META_PAD = CHUNK - N_META
D_FF = 2816
LN_EPS = 1e-5
DEEPNORM_ALPHA = (2.0 * DEPTH) ** 0.25
DEEPNORM_BETA = (8.0 * DEPTH) ** -0.25

GDN_HEADS = 4
GDN_DK = 128
GDN_DV = 128
GDN_CONV = 4
RWKV_HEADS = 4
RWKV_HEAD = 64
RWKV_DECAY_RANK = 32
RWKV_A_RANK = 32
RWKV_GATE_RANK = 64
RWKV_LNX_EPS = 64e-5
RET_HEADS = 4
RET_DK = 32
RET_DV = 64
ROPE_BASE = 10000.0

D_A = GDN_HEADS * GDN_DV
D_B = RWKV_HEADS * RWKV_HEAD
D_C = RET_HEADS * RET_DV
D_MIX = D_A + D_B + D_C
GDN_QKV = 2 * GDN_HEADS * GDN_DK + GDN_HEADS * GDN_DV
D_A_IN = GDN_QKV + D_A + 2 * GDN_HEADS
D_B_IN = 3 * D_B + RWKV_DECAY_RANK + RWKV_A_RANK + RWKV_GATE_RANK
D_C_IN = 2 * RET_HEADS * RET_DK + 2 * D_C
D_IN = D_A_IN + D_B_IN + D_C_IN
GDN_SPLITS = (GDN_QKV, GDN_QKV + D_A, GDN_QKV + D_A + GDN_HEADS)
RWKV_SPLITS = (D_B, 2 * D_B, 3 * D_B, 3 * D_B + RWKV_DECAY_RANK,
               3 * D_B + RWKV_DECAY_RANK + RWKV_A_RANK)
RET_SPLITS = (RET_HEADS * RET_DK, 2 * RET_HEADS * RET_DK, 2 * RET_HEADS * RET_DK + D_C)

kernel_name = 'hybrid_gdn_rwkv7_retnet_macaron_deepnorm'


def layer_norm(x, g, b):
    xf = x.astype(jnp.float32)
    mu = xf.mean(-1, keepdims=True)
    var = jnp.square(xf - mu).mean(-1, keepdims=True)
    return ((xf - mu) * lax.rsqrt(var + LN_EPS)).astype(x.dtype) * g + b


def head_group_norm(y, g, b, eps):
    yf = y.astype(jnp.float32)
    mu = yf.mean(-1, keepdims=True)
    var = jnp.square(yf - mu).mean(-1, keepdims=True)
    yn = (yf - mu) * lax.rsqrt(var + eps)
    return yn.reshape(y.shape[0], y.shape[1], -1) * g + b


def l2_normalize(x):
    xf = x.astype(jnp.float32)
    return xf * lax.rsqrt(jnp.sum(xf * xf, -1, keepdims=True) + 1e-6)


def swiglu(h, w_in, w_out):
    gate, up = jnp.split(h @ w_in, 2, axis=-1)
    return (jax.nn.silu(gate) * up) @ w_out


def causal_depthwise_conv(x, w):
    k_len, ch = w.shape
    return lax.conv_general_dilated(x, w[:, None, :].astype(x.dtype), window_strides=(1,),
                                    padding=[(k_len - 1, 0)],
                                    dimension_numbers=('NWC', 'WIO', 'NWC'),
                                    feature_group_count=ch)


def rotary(x):
    L, d = x.shape[1], x.shape[-1]
    half = d // 2
    inv_freq = 1.0 / (ROPE_BASE ** jnp.linspace(0.0, 1.0, half, dtype=jnp.float32))
    ang = jnp.arange(L, dtype=jnp.float32)[:, None] * inv_freq
    cos, sin = jnp.cos(ang)[None, :, None, :], jnp.sin(ang)[None, :, None, :]
    xf = x.astype(jnp.float32)
    x1, x2 = xf[..., :half], xf[..., half:]
    return jnp.concatenate([x1 * cos - x2 * sin, x1 * sin + x2 * cos], -1)


def to_chunks(x):
    b, L, h = x.shape[:3]
    x = x.reshape(b, L // CHUNK, CHUNK, h, *x.shape[3:])
    return jnp.moveaxis(x, (1, 3), (0, 2))


def from_chunks(x):
    x = jnp.moveaxis(x, (0, 2), (1, 3))
    b, n, c, h = x.shape[:4]
    return x.reshape(b, n * c, h, *x.shape[4:])


def chunked_gated_delta_rule(q, k, v, log_g, beta):
    dk, dv = q.shape[-1], v.shape[-1]
    q = to_chunks(q.astype(jnp.float32)) * dk ** -0.5
    k = to_chunks(k.astype(jnp.float32))
    v = to_chunks(v.astype(jnp.float32))
    log_g = to_chunks(log_g.astype(jnp.float32))
    beta = to_chunks(beta.astype(jnp.float32))
    G = jnp.cumsum(log_g, axis=-1)
    causal = jnp.tril(jnp.ones((CHUNK, CHUNK), bool))
    strict = jnp.tril(jnp.ones((CHUNK, CHUNK), bool), -1)
    diff = G[..., :, None] - G[..., None, :]
    decay = jnp.where(causal, jnp.exp(jnp.where(causal, diff, 0.0)), 0.0)
    k_beta = k * beta[..., None]
    a_mat = jnp.where(strict, jnp.einsum('nbhcd,nbhmd->nbhcm', k_beta, k) * decay, 0.0)
    rhs = jnp.concatenate([v * beta[..., None], k_beta * jnp.exp(G)[..., None]], -1)
    sol = lax.linalg.triangular_solve(a_mat, rhs, left_side=True, lower=True,
                                      unit_diagonal=True)
    u, w = sol[..., :dv], sol[..., dv:]
    qk = jnp.einsum('nbhcd,nbhmd->nbhcm', q, k) * decay
    q_dec = q * jnp.exp(G)[..., None]
    g_last = G[..., -1]
    k_dec = k * jnp.exp(g_last[..., None] - G)[..., None]

    def step(S, xs):
        q_i, qk_i, u_i, w_i, k_i, gl_i = xs
        v_new = u_i - jnp.einsum('bhcd,bhde->bhce', w_i, S)
        o_i = jnp.einsum('bhcd,bhde->bhce', q_i, S) + jnp.einsum('bhcm,bhme->bhce', qk_i, v_new)
        S = S * jnp.exp(gl_i)[..., None, None] + jnp.einsum('bhcd,bhce->bhde', k_i, v_new)
        return S, o_i

    S0 = jnp.zeros(q.shape[1:3] + (dk, dv), jnp.float32)
    _, o = lax.scan(step, S0, (q_dec, qk, u, w, k_dec, g_last))
    return from_chunks(o)


def gated_deltanet(p, conv_w, a_log, dt_bias, norm_w):
    b, L, _ = p.shape
    qkv, z, b_raw, a_raw = jnp.split(p, GDN_SPLITS, axis=-1)
    qkv = jax.nn.silu(causal_depthwise_conv(qkv, conv_w))
    q, k, v = jnp.split(qkv, (GDN_HEADS * GDN_DK, 2 * GDN_HEADS * GDN_DK), axis=-1)
    q = l2_normalize(q.reshape(b, L, GDN_HEADS, GDN_DK))
    k = l2_normalize(k.reshape(b, L, GDN_HEADS, GDN_DK))
    v = v.reshape(b, L, GDN_HEADS, GDN_DV)
    beta = jax.nn.sigmoid(b_raw.astype(jnp.float32))
    log_g = -jnp.exp(a_log) * jax.nn.softplus(a_raw.astype(jnp.float32) + dt_bias)
    o = chunked_gated_delta_rule(q, k, v, log_g, beta)
    o = o * lax.rsqrt(jnp.mean(o * o, -1, keepdims=True) + LN_EPS) * norm_w
    o = o * jax.nn.silu(z.astype(jnp.float32).reshape(b, L, GDN_HEADS, GDN_DV))
    return o.reshape(b, L, D_A).astype(p.dtype)


def rwkv7_scan(r, decay, k, v, a, b):
    xs = tuple(jnp.moveaxis(t.astype(jnp.float32), 1, 0) for t in (r, decay, k, v, a, b))

    def step(S, xt):
        r_t, w_t, k_t, v_t, a_t, b_t = xt
        sa = jnp.einsum('bhvk,bhk->bhv', S, a_t)
        S = S * w_t[:, :, None, :] + sa[..., None] * b_t[:, :, None, :] + v_t[..., None] * k_t[:, :, None, :]
        return S, jnp.einsum('bhvk,bhk->bhv', S, r_t)

    bsz, _, h, n = r.shape
    _, y = lax.scan(step, jnp.zeros((bsz, h, n, n), jnp.float32), xs)
    return jnp.moveaxis(y, 0, 1)


def rwkv7_mixer(p, mu, w0, w_up, a0, a_up, g_up, k_k, k_a, r_k, lnx_g, lnx_b):
    b, L, _ = p.shape
    pf = p.astype(jnp.float32)
    prev = jnp.pad(pf, ((0, 0), (1, 0), (0, 0)))[:, :-1]
    pf = pf + (prev - pf) * mu
    r, k, v, xw, xa, xg = jnp.split(pf, RWKV_SPLITS, axis=-1)
    log_w = -jax.nn.softplus(-(w0 + jnp.tanh(xw) @ w_up)) - 0.5
    decay = jnp.exp(-jnp.exp(log_w))
    a = jax.nn.sigmoid(a0 + xa @ a_up)
    g = jax.nn.sigmoid(xg) @ g_up
    heads = lambda t: t.reshape(b, L, RWKV_HEADS, RWKV_HEAD)
    kk = l2_normalize(heads(k * k_k))
    k = k * (1.0 + (a - 1.0) * k_a)
    r_h, k_h, v_h, a_h = heads(r), heads(k), heads(v), heads(a)
    y = rwkv7_scan(r_h, heads(decay), k_h, v_h, -kk, kk * a_h)
    y = head_group_norm(y, lnx_g, lnx_b, RWKV_LNX_EPS)
    y = y + (jnp.sum(r_h * k_h * r_k, -1, keepdims=True) * v_h).reshape(b, L, D_B)
    return (y * g).astype(p.dtype)


def chunked_retention(q, k, v):
    dk = q.shape[-1]
    q = to_chunks(q.astype(jnp.float32))
    k = to_chunks(k.astype(jnp.float32)) * dk ** -0.5
    v = to_chunks(v.astype(jnp.float32))
    log_gamma = jnp.log(1.0 - 2.0 ** (-5.0 - jnp.arange(RET_HEADS, dtype=jnp.float32)))
    idx = jnp.arange(CHUNK, dtype=jnp.float32)
    diff = idx[:, None] - idx[None, :]
    d_intra = jnp.where(diff >= 0, jnp.exp(log_gamma[:, None, None] * jnp.maximum(diff, 0.0)), 0.0)
    intra = jnp.einsum('nbhcm,nbhme->nbhce', jnp.einsum('nbhcd,nbhmd->nbhcm', q, k) * d_intra, v)
    q_dec = q * jnp.exp(log_gamma[:, None] * (idx + 1.0))[..., None]
    k_dec = k * jnp.exp(log_gamma[:, None] * (CHUNK - 1.0 - idx))[..., None]
    chunk_decay = jnp.exp(log_gamma * CHUNK)[:, None, None]

    def step(S, xs):
        q_i, k_i, v_i = xs
        o_i = jnp.einsum('bhcd,bhde->bhce', q_i, S)
        S = S * chunk_decay + jnp.einsum('bhcd,bhce->bhde', k_i, v_i)
        return S, o_i

    S0 = jnp.zeros(q.shape[1:3] + (dk, v.shape[-1]), jnp.float32)
    _, cross = lax.scan(step, S0, (q_dec, k_dec, v))
    return from_chunks(intra + cross)


def retention_mixer(p, norm_g, norm_b):
    b, L, _ = p.shape
    q, k, v, g = jnp.split(p, RET_SPLITS, axis=-1)
    q = rotary(q.reshape(b, L, RET_HEADS, RET_DK))
    k = rotary(k.reshape(b, L, RET_HEADS, RET_DK))
    y = chunked_retention(q, k, v.reshape(b, L, RET_HEADS, RET_DV))
    y = head_group_norm(y, norm_g, norm_b, LN_EPS) * jax.nn.silu(g.astype(jnp.float32))
    return y.astype(p.dtype)


def hybrid_mixer(h, w_in, w_out, gdn_conv_w, gdn_a_log, gdn_dt_bias, gdn_norm_w,
                 rwkv_mu, rwkv_w0, rwkv_w_up, rwkv_a0, rwkv_a_up, rwkv_g_up, rwkv_k_k,
                 rwkv_k_a, rwkv_r_k, rwkv_lnx_g, rwkv_lnx_b, ret_norm_g, ret_norm_b):
    proj = h @ w_in
    pa, pb, pc = jnp.split(proj, (D_A_IN, D_A_IN + D_B_IN), axis=-1)
    ya = gated_deltanet(pa, gdn_conv_w, gdn_a_log, gdn_dt_bias, gdn_norm_w)
    yb = rwkv7_mixer(pb, rwkv_mu, rwkv_w0, rwkv_w_up, rwkv_a0, rwkv_a_up, rwkv_g_up,
                     rwkv_k_k, rwkv_k_a, rwkv_r_k, rwkv_lnx_g, rwkv_lnx_b)
    yc = retention_mixer(pc, ret_norm_g, ret_norm_b)
    return jnp.concatenate([ya, yb, yc], axis=-1) @ w_out


def setup_inputs(seed: int = 0) -> dict:
    key = jax.random.key(seed)
    ks = jax.random.split(key, 28)
    f32 = jnp.float32
    nrm = lambda k, shape, scale: jax.random.normal(k, shape, f32) * scale
    dt = jnp.exp(jax.random.uniform(ks[12], (DEPTH, GDN_HEADS), f32, math.log(1e-3), math.log(1e-1)))
    return {
        'x': nrm(ks[0], (BATCH, SEQ, D_MODEL), 1.0),
        'meta_tokens': nrm(ks[1], (N_META, D_MODEL), 1.0),
        'ln_g': 1.0 + nrm(ks[2], (DEPTH, 3, D_MODEL), 0.02),
        'ln_b': nrm(ks[3], (DEPTH, 3, D_MODEL), 0.02),
        'w_ff1_in': nrm(ks[4], (DEPTH, D_MODEL, 2 * D_FF), D_MODEL ** -0.5),
        'w_ff1_out': nrm(ks[5], (DEPTH, D_FF, D_MODEL), DEEPNORM_BETA * D_FF ** -0.5),
        'w_ff2_in': nrm(ks[6], (DEPTH, D_MODEL, 2 * D_FF), D_MODEL ** -0.5),
        'w_ff2_out': nrm(ks[7], (DEPTH, D_FF, D_MODEL), DEEPNORM_BETA * D_FF ** -0.5),
        'w_in': nrm(ks[8], (DEPTH, D_MODEL, D_IN), D_MODEL ** -0.5),
        'w_out': nrm(ks[9], (DEPTH, D_MIX, D_MODEL), DEEPNORM_BETA * D_MIX ** -0.5),
        'gdn_conv_w': nrm(ks[10], (DEPTH, GDN_CONV, GDN_QKV), GDN_CONV ** -0.5),
        'gdn_a_log': jnp.log(jax.random.uniform(ks[11], (DEPTH, GDN_HEADS), f32, 1.0, 16.0)),
        'gdn_dt_bias': dt + jnp.log(-jnp.expm1(-dt)),
        'gdn_norm_w': 1.0 + nrm(ks[13], (DEPTH, GDN_DV), 0.02),
        'rwkv_mu': jax.random.uniform(ks[14], (DEPTH, D_B_IN), f32),
        'rwkv_w0': jax.random.uniform(ks[15], (DEPTH, D_B), f32, -6.5, -1.0),
        'rwkv_w_up': nrm(ks[16], (DEPTH, RWKV_DECAY_RANK, D_B), RWKV_DECAY_RANK ** -0.5),
        'rwkv_a0': nrm(ks[17], (DEPTH, D_B), 0.1),
        'rwkv_a_up': nrm(ks[18], (DEPTH, RWKV_A_RANK, D_B), RWKV_A_RANK ** -0.5),
        'rwkv_g_up': nrm(ks[19], (DEPTH, RWKV_GATE_RANK, D_B), RWKV_GATE_RANK ** -0.5),
        'rwkv_k_k': 0.85 + nrm(ks[20], (DEPTH, D_B), 0.02),
        'rwkv_k_a': 1.0 + nrm(ks[21], (DEPTH, D_B), 0.02),
        'rwkv_r_k': nrm(ks[22], (DEPTH, RWKV_HEADS, RWKV_HEAD), 0.1),
        'rwkv_lnx_g': 1.0 + nrm(ks[23], (DEPTH, D_B), 0.02),
        'rwkv_lnx_b': nrm(ks[24], (DEPTH, D_B), 0.02),
        'ret_norm_g': 1.0 + nrm(ks[25], (DEPTH, D_C), 0.02),
        'ret_norm_b': nrm(ks[26], (DEPTH, D_C), 0.02),
    }


def reference(x, meta_tokens, ln_g, ln_b, w_ff1_in, w_ff1_out, w_ff2_in, w_ff2_out, w_in, w_out,
              gdn_conv_w, gdn_a_log, gdn_dt_bias, gdn_norm_w, rwkv_mu, rwkv_w0, rwkv_w_up,
              rwkv_a0, rwkv_a_up, rwkv_g_up, rwkv_k_k, rwkv_k_a, rwkv_r_k, rwkv_lnx_g, rwkv_lnx_b,
              ret_norm_g, ret_norm_b):
    bsz, _, d = x.shape
    pad = jnp.zeros((bsz, META_PAD, d), x.dtype)
    meta = jnp.broadcast_to(meta_tokens.astype(x.dtype)[None], (bsz, N_META, d))
    h = jnp.concatenate([pad, meta, x], axis=1)
    L = h.shape[1]
    valid = (jnp.arange(L) >= META_PAD).astype(h.dtype)[None, :, None]
    for l in range(DEPTH):
        h = layer_norm(DEEPNORM_ALPHA * h + 0.5 * swiglu(h, w_ff1_in[l], w_ff1_out[l]), ln_g[l, 0], ln_b[l, 0])
        mix = hybrid_mixer(h * valid, w_in[l], w_out[l], gdn_conv_w[l], gdn_a_log[l], gdn_dt_bias[l],
                           gdn_norm_w[l], rwkv_mu[l], rwkv_w0[l], rwkv_w_up[l], rwkv_a0[l], rwkv_a_up[l],
                           rwkv_g_up[l], rwkv_k_k[l], rwkv_k_a[l], rwkv_r_k[l], rwkv_lnx_g[l],
                           rwkv_lnx_b[l], ret_norm_g[l], ret_norm_b[l])
        h = layer_norm(DEEPNORM_ALPHA * h + mix, ln_g[l, 1], ln_b[l, 1])
        h = layer_norm(DEEPNORM_ALPHA * h + 0.5 * swiglu(h, w_ff2_in[l], w_ff2_out[l]), ln_g[l, 2], ln_b[l, 2])
    return h[:, CHUNK:]
```

```python
import functools
import math

import jax
import jax.numpy as jnp
from jax import lax
from jax.experimental import pallas as pl
from jax.experimental.pallas import tpu as pltpu

F32 = jnp.float32
BF16 = jnp.bfloat16
HIGHEST = lax.Precision.HIGHEST

D_MODEL = 1024
DEPTH = 2
CHUNK = 64
N_META = 16
META_PAD = CHUNK - N_META
D_FF = 2816
LN_EPS = 1e-5
ALPHA = (2.0 * DEPTH) ** 0.25

GDN_HEADS, GDN_DK, GDN_DV, GDN_CONV = 4, 128, 128, 4
GDN_QKV = 2 * GDN_HEADS * GDN_DK + GDN_HEADS * GDN_DV
D_A = GDN_HEADS * GDN_DV
GATE_PAD = 128
D_A_COLS = GDN_QKV + D_A + GATE_PAD
RWKV_HEADS, RWKV_HEAD = 4, 64
RWKV_DECAY_RANK, RWKV_A_RANK, RWKV_GATE_RANK = 32, 32, 64
RWKV_LNX_EPS = 64e-5
D_B = RWKV_HEADS * RWKV_HEAD
D_B_IN = 3 * D_B + RWKV_DECAY_RANK + RWKV_A_RANK + RWKV_GATE_RANK
RET_HEADS, RET_DK, RET_DV = 4, 32, 64
ROPE_BASE = 10000.0
D_C = RET_HEADS * RET_DV
D_C_IN = 2 * RET_HEADS * RET_DK + 2 * D_C
D_A_IN = GDN_QKV + D_A + 2 * GDN_HEADS

VMEM_LIMIT_BYTES = 56 * 1024 * 1024
ROW_TILE = 768
FF_CHUNKS = ((0, 1024), (1024, 1024), (2048, 768))


def _dot(a, b, precision=None):
    return jnp.dot(a, b, preferred_element_type=F32, precision=precision)


def _dot_nt(a, b, precision=None):
    return lax.dot_general(a, b, (((1,), (1,)), ((), ())), preferred_element_type=F32,
                           precision=precision)


def _dot_tn(a, b, precision=None):
    return lax.dot_general(a, b, (((0,), (0,)), ((), ())), preferred_element_type=F32,
                           precision=precision)


def _bdot(a, b):
    return _dot(a.astype(BF16), b.astype(BF16))


def _bdot_nt(a, b):
    return _dot_nt(a.astype(BF16), b.astype(BF16))


def _bdot_tn(a, b):
    return _dot_tn(a.astype(BF16), b.astype(BF16))


def _layer_norm(y, g, b):
    mu = jnp.mean(y, axis=-1, keepdims=True)
    d = y - mu
    var = jnp.mean(d * d, axis=-1, keepdims=True)
    return d * lax.rsqrt(var + LN_EPS) * g + b


def _silu(x):
    return x * jax.nn.sigmoid(x)


def _softplus(x):
    return jnp.maximum(x, 0.0) + jnp.log(1.0 + jnp.exp(-jnp.abs(x)))


def _tri_masks():
    row = lax.broadcasted_iota(jnp.int32, (CHUNK, CHUNK), 0)
    col = lax.broadcasted_iota(jnp.int32, (CHUNK, CHUNK), 1)
    return row >= col, row > col, row == col


def _nilpotent_series(x, eye):
    p = eye + x
    for _ in range(5):
        x = _dot(x, x, HIGHEST)
        p = p + _dot(p, x, HIGHEST)
    return p


def _ffn_ln_kernel(x_ref, win_ref, wout_ref, g_ref, b_ref, o_ref):
    x = x_ref[...]
    xb = x.astype(BF16)
    acc = jnp.zeros(x.shape, F32)
    for start, width in FF_CHUNKS:
        gate = _dot(xb, win_ref[:, start:start + width])
        up = _dot(xb, win_ref[:, D_FF + start:D_FF + start + width])
        act = (_silu(gate) * up).astype(BF16)
        acc = acc + _dot(act, wout_ref[start:start + width, :])
    o_ref[...] = _layer_norm(ALPHA * x + 0.5 * acc, g_ref[...], b_ref[...])


def _const_spec(shape):
    return pl.BlockSpec(shape, lambda *_: (0,) * len(shape), pipeline_mode=pl.Buffered(1))


def _ffn_ln(h, w_in, w_out, g, b):
    rows = h.shape[0]
    tile = ROW_TILE if rows % ROW_TILE == 0 else rows
    return pl.pallas_call(
        _ffn_ln_kernel,
        grid=(rows // tile,),
        in_specs=[pl.BlockSpec((tile, D_MODEL), lambda i: (i, 0)),
                  _const_spec((D_MODEL, 2 * D_FF)),
                  _const_spec((D_FF, D_MODEL)),
                  _const_spec((1, D_MODEL)),
                  _const_spec((1, D_MODEL))],
        out_specs=pl.BlockSpec((tile, D_MODEL), lambda i: (i, 0)),
        out_shape=jax.ShapeDtypeStruct((rows, D_MODEL), F32),
        compiler_params=pltpu.CompilerParams(dimension_semantics=("parallel",),
                                             vmem_limit_bytes=VMEM_LIMIT_BYTES),
        name="ffn_ln",
    )(h, w_in, w_out, g, b)


def _in_proj_kernel(h_ref, wa_ref, wb_ref, wc_ref, pa_ref, pb_ref, pc_ref, *, tile):
    pos = pl.program_id(1) * tile + lax.broadcasted_iota(jnp.int32, (tile, 1), 0)
    hb = jnp.where(pos >= META_PAD, h_ref[0], 0.0).astype(BF16)
    pa_ref[0] = _dot(hb, wa_ref[...])
    pb_ref[0] = _dot(hb, wb_ref[...])
    pc_ref[0] = _dot(hb, wc_ref[...])


def _seq_tile(seq):
    for cand in (1032, 688, 344, CHUNK):
        if seq % cand == 0:
            return cand
    return seq


def _in_proj(h, wa, wb, wc):
    bsz, seq, _ = h.shape
    tile = _seq_tile(seq)
    row = lambda width: pl.BlockSpec((1, tile, width), lambda b, j: (b, j, 0))
    return pl.pallas_call(
        functools.partial(_in_proj_kernel, tile=tile),
        grid=(bsz, seq // tile),
        in_specs=[row(D_MODEL), _const_spec((D_MODEL, D_A_COLS)), _const_spec((D_MODEL, D_B_IN)),
                  _const_spec((D_MODEL, D_C_IN))],
        out_specs=[row(D_A_COLS), row(D_B_IN), row(D_C_IN)],
        out_shape=[jax.ShapeDtypeStruct((bsz, seq, D_A_COLS), F32),
                   jax.ShapeDtypeStruct((bsz, seq, D_B_IN), F32),
                   jax.ShapeDtypeStruct((bsz, seq, D_C_IN), F32)],
        compiler_params=pltpu.CompilerParams(dimension_semantics=("parallel", "parallel"),
                                             vmem_limit_bytes=VMEM_LIMIT_BYTES),
        name="in_proj",
    )(h, wa, wb, wc)


def _out_proj_ln_kernel(h_ref, ya_ref, yb_ref, yc_ref, w_ref, g_ref, b_ref, o_ref):
    mix = _dot(ya_ref[...], w_ref[0:D_A, :])
    mix = mix + _dot(yb_ref[...], w_ref[D_A:D_A + D_B, :])
    mix = mix + _dot(yc_ref[...], w_ref[D_A + D_B:, :])
    o_ref[...] = _layer_norm(ALPHA * h_ref[...] + mix, g_ref[...], b_ref[...])


def _out_proj_ln(h, ya, yb, yc, w_out, g, b):
    rows = h.shape[0]
    tile = ROW_TILE if rows % ROW_TILE == 0 else rows
    row = lambda width: pl.BlockSpec((tile, width), lambda i: (i, 0))
    return pl.pallas_call(
        _out_proj_ln_kernel,
        grid=(rows // tile,),
        in_specs=[row(D_MODEL), row(D_A), row(D_B), row(D_C),
                  _const_spec((D_A + D_B + D_C, D_MODEL)),
                  _const_spec((1, D_MODEL)), _const_spec((1, D_MODEL))],
        out_specs=row(D_MODEL),
        out_shape=jax.ShapeDtypeStruct((rows, D_MODEL), F32),
        compiler_params=pltpu.CompilerParams(dimension_semantics=("parallel",),
                                             vmem_limit_bytes=VMEM_LIMIT_BYTES),
        name="out_proj_ln",
    )(h, ya, yb, yc, w_out, g, b)


def _gdn_kernel(pa_ref, conv_ref, alog_ref, dt_ref, nw_ref, o_ref, cbuf, state):
    @pl.when(pl.program_id(1) == 0)
    def _():
        cbuf[0:8, :] = jnp.zeros((8, GDN_QKV), F32)
        state[...] = jnp.zeros(state.shape, F32)

    causal, strict, diag = _tri_masks()
    tril = causal.astype(F32)
    strict_f = strict.astype(F32)
    eye = diag.astype(F32)

    x = pa_ref[0, :, 0:GDN_QKV]
    cbuf[8:8 + CHUNK, :] = x
    y = conv_ref[3:4, :] * x
    for tap in range(GDN_CONV - 1):
        lo = 8 - (GDN_CONV - 1) + tap
        y = y + conv_ref[tap:tap + 1, :] * cbuf[lo:lo + CHUNK, :]
    cbuf[0:8, :] = x[CHUNK - 8:, :]
    qkv = _silu(y)

    gates = pa_ref[0, :, GDN_QKV + D_A:]
    beta_all = jax.nn.sigmoid(gates)
    lg_all = -jnp.exp(alog_ref[...]) * _softplus(gates + dt_ref[...])
    g_all = _dot(tril, lg_all, HIGHEST)

    for h in range(GDN_HEADS):
        q = qkv[:, h * GDN_DK:(h + 1) * GDN_DK]
        k = qkv[:, (GDN_HEADS + h) * GDN_DK:(GDN_HEADS + h + 1) * GDN_DK]
        v = qkv[:, 2 * GDN_HEADS * GDN_DK + h * GDN_DV:2 * GDN_HEADS * GDN_DK + (h + 1) * GDN_DV]
        q = q * lax.rsqrt(jnp.sum(q * q, -1, keepdims=True) + 1e-6) * GDN_DK ** -0.5
        k = k * lax.rsqrt(jnp.sum(k * k, -1, keepdims=True) + 1e-6)
        beta = beta_all[:, h:h + 1]
        lg = lg_all[:, GDN_HEADS + h:GDN_HEADS + h + 1]
        gc = g_all[:, GDN_HEADS + h:GDN_HEADS + h + 1]
        diff = _dot(tril, lg * strict_f, HIGHEST)
        decay = jnp.where(causal, jnp.exp(jnp.where(causal, diff, 0.0)), 0.0)
        kb = k * beta
        a_mat = jnp.where(strict, _bdot_nt(kb, k) * decay, 0.0)
        t_inv = _nilpotent_series(-a_mat, eye)
        e_g = jnp.exp(gc)
        rhs = jnp.concatenate([v * beta, kb * e_g], axis=1)
        sol = _dot(t_inv, rhs, HIGHEST)
        u, w = sol[:, :GDN_DV], sol[:, GDN_DV:]
        qk = _bdot_nt(q, k) * decay
        g_last = gc[CHUNK - 1:CHUNK, :]
        k_dec = k * jnp.exp(g_last - gc)
        s0 = state[h]
        ws = _bdot(jnp.concatenate([w, q * e_g], axis=0), s0)
        v_new = u - ws[:CHUNK]
        o = ws[CHUNK:] + _bdot(qk, v_new)
        state[h] = s0 * jnp.exp(g_last) + _bdot_tn(k_dec, v_new)
        o = o * lax.rsqrt(jnp.mean(o * o, -1, keepdims=True) + LN_EPS) * nw_ref[...]
        z = pa_ref[0, :, GDN_QKV + h * GDN_DV:GDN_QKV + (h + 1) * GDN_DV]
        o_ref[0, :, h * GDN_DV:(h + 1) * GDN_DV] = (o * _silu(z)).astype(o_ref.dtype)


def _gdn_mixer(pa, conv_w, a_log_pad, dt_pad, norm_w):
    bsz, seq, _ = pa.shape
    return pl.pallas_call(
        _gdn_kernel,
        grid=(bsz, seq // CHUNK),
        in_specs=[pl.BlockSpec((1, CHUNK, D_A_COLS), lambda b, c: (b, c, 0)),
                  _const_spec((GDN_CONV, GDN_QKV)), _const_spec((1, GATE_PAD)),
                  _const_spec((1, GATE_PAD)), _const_spec((1, GDN_DV))],
        out_specs=pl.BlockSpec((1, CHUNK, D_A), lambda b, c: (b, c, 0)),
        out_shape=jax.ShapeDtypeStruct((bsz, seq, D_A), BF16),
        scratch_shapes=[pltpu.VMEM((8 + CHUNK, GDN_QKV), F32),
                        pltpu.VMEM((GDN_HEADS, GDN_DK, GDN_DV), F32)],
        compiler_params=pltpu.CompilerParams(dimension_semantics=("parallel", "arbitrary"),
                                             vmem_limit_bytes=VMEM_LIMIT_BYTES),
        name="gdn_mixer",
    )(pa, conv_w, a_log_pad, dt_pad, norm_w)


def _rwkv_kernel(pb_ref, mu_ref, w0_ref, wup_ref, a0_ref, aup_ref, gup_ref, kk_ref, ka_ref,
                 rk_ref, lg_ref, lb_ref, o_ref, sbuf, state):
    @pl.when(pl.program_id(1) == 0)
    def _():
        sbuf[0:8, :] = jnp.zeros((8, D_B_IN), F32)
        state[...] = jnp.zeros(state.shape, F32)

    causal, strict, diag = _tri_masks()
    tril = causal.astype(F32)
    eye = diag.astype(F32)

    x = pb_ref[0]
    sbuf[8:8 + CHUNK, :] = x
    prev = sbuf[7:7 + CHUNK, :]
    sbuf[0:8, :] = x[CHUNK - 8:, :]
    pf = x + (prev - x) * mu_ref[...]
    r_all, k_all, v_all = pf[:, 0:D_B], pf[:, D_B:2 * D_B], pf[:, 2 * D_B:3 * D_B]
    c0 = 3 * D_B
    xw = pf[:, c0:c0 + RWKV_DECAY_RANK]
    xa = pf[:, c0 + RWKV_DECAY_RANK:c0 + RWKV_DECAY_RANK + RWKV_A_RANK]
    xg = pf[:, c0 + RWKV_DECAY_RANK + RWKV_A_RANK:]
    log_w = -_softplus(-(w0_ref[...] + _dot(jnp.tanh(xw), wup_ref[...], HIGHEST))) - 0.5
    wlog_all = -jnp.exp(log_w)
    a_all = jax.nn.sigmoid(a0_ref[...] + _dot(xa, aup_ref[...], HIGHEST))
    gate_all = _bdot(jax.nn.sigmoid(xg), gup_ref[...])
    kk_all = k_all * kk_ref[...]
    k2_all = k_all * (1.0 + (a_all - 1.0) * ka_ref[...])
    gcum_all = _dot(tril, wlog_all, HIGHEST)

    outs = []
    for h in range(RWKV_HEADS):
        sl = slice(h * RWKV_HEAD, (h + 1) * RWKV_HEAD)
        r, k2, v, a, wlog, gc = r_all[:, sl], k2_all[:, sl], v_all[:, sl], a_all[:, sl], wlog_all[:, sl], gcum_all[:, sl]
        kk = kk_all[:, sl]
        kk = kk * lax.rsqrt(jnp.sum(kk * kk, -1, keepdims=True) + 1e-6)
        bb = kk * a
        e_neg = jnp.exp(-gc)
        r_t = r * jnp.exp(gc)
        a_t = -kk * jnp.exp(gc - wlog)
        k_t = k2 * e_neg
        b_t = bb * e_neg
        big = _dot_nt(jnp.concatenate([a_t, r_t], axis=0), jnp.concatenate([b_t, k_t], axis=0), HIGHEST)
        a_ab = jnp.where(strict, big[:CHUNK, :CHUNK], 0.0)
        a_ak = jnp.where(strict, big[:CHUNK, CHUNK:], 0.0)
        a_rb = jnp.where(causal, big[CHUNK:, :CHUNK], 0.0)
        a_rk = jnp.where(causal, big[CHUNK:, CHUNK:], 0.0)
        m_inv = _nilpotent_series(a_ab, eye)
        wu = _dot(m_inv, jnp.concatenate([a_t, _dot(a_ak, v, HIGHEST)], axis=1), HIGHEST)
        w_mat, u2 = wu[:, :RWKV_HEAD], wu[:, RWKV_HEAD:]
        rkv = _dot(a_rk, v, HIGHEST)
        g_last = gc[CHUNK - 1:CHUNK, :]
        e_tail = jnp.exp(g_last - gc)
        s0 = state[h]
        wr = _dot_nt(jnp.concatenate([w_mat, r_t], axis=0), s0, HIGHEST)
        u = wr[:CHUNK] + u2
        y = wr[CHUNK:] + _dot(a_rb, u, HIGHEST) + rkv
        state[h] = s0 * jnp.exp(g_last) + _dot_tn(jnp.concatenate([u, v], axis=0),
                                                  jnp.concatenate([bb * e_tail, k2 * e_tail], axis=0), HIGHEST)
        mu = jnp.mean(y, -1, keepdims=True)
        d = y - mu
        var = jnp.mean(d * d, -1, keepdims=True)
        yn = d * lax.rsqrt(var + RWKV_LNX_EPS) * lg_ref[:, sl] + lb_ref[:, sl]
        yn = yn + jnp.sum(r * k2 * rk_ref[:, sl], -1, keepdims=True) * v
        outs.append(yn * gate_all[:, sl])
    o_ref[0] = jnp.concatenate(outs, axis=1).astype(o_ref.dtype)


def _rwkv_mixer(pb, mu, w0, w_up, a0, a_up, g_up, k_k, k_a, r_k, lnx_g, lnx_b):
    bsz, seq, _ = pb.shape
    vec = _const_spec((1, D_B))
    return pl.pallas_call(
        _rwkv_kernel,
        grid=(bsz, seq // CHUNK),
        in_specs=[pl.BlockSpec((1, CHUNK, D_B_IN), lambda b, c: (b, c, 0)),
                  _const_spec((1, D_B_IN)), vec, _const_spec((RWKV_DECAY_RANK, D_B)), vec,
                  _const_spec((RWKV_A_RANK, D_B)), _const_spec((RWKV_GATE_RANK, D_B)),
                  vec, vec, vec, vec, vec],
        out_specs=pl.BlockSpec((1, CHUNK, D_B), lambda b, c: (b, c, 0)),
        out_shape=jax.ShapeDtypeStruct((bsz, seq, D_B), BF16),
        scratch_shapes=[pltpu.VMEM((8 + CHUNK, D_B_IN), F32),
                        pltpu.VMEM((RWKV_HEADS, RWKV_HEAD, RWKV_HEAD), F32)],
        compiler_params=pltpu.CompilerParams(dimension_semantics=("parallel", "arbitrary"),
                                             vmem_limit_bytes=VMEM_LIMIT_BYTES),
        name="rwkv_mixer",
    )(pb, mu, w0, w_up, a0, a_up, g_up, k_k, k_a, r_k, lnx_g, lnx_b)


def _ret_log_gamma(h):
    return math.log(1.0 - 2.0 ** (-5.0 - h))


def _ret_kernel(pc_ref, cos_ref, sin_ref, ng_ref, nb_ref, o_ref, state):
    @pl.when(pl.program_id(1) == 0)
    def _():
        state[...] = jnp.zeros(state.shape, F32)

    qk_w = RET_HEADS * RET_DK
    lane = lax.broadcasted_iota(jnp.int32, (CHUNK, qk_w), 1)
    first_half = (lane % RET_DK) < (RET_DK // 2)

    def rope(t):
        partner = jnp.where(first_half, pltpu.roll(t, qk_w - RET_DK // 2, 1), pltpu.roll(t, RET_DK // 2, 1))
        return t * cos_ref[...] + partner * sin_ref[...]

    q_all = rope(pc_ref[0, :, 0:qk_w])
    k_all = rope(pc_ref[0, :, qk_w:2 * qk_w]) * RET_DK ** -0.5
    v_all = pc_ref[0, :, 2 * qk_w:2 * qk_w + D_C]
    gate_all = pc_ref[0, :, 2 * qk_w + D_C:]

    row = lax.broadcasted_iota(jnp.int32, (CHUNK, CHUNK), 0)
    col = lax.broadcasted_iota(jnp.int32, (CHUNK, CHUNK), 1)
    delta = (row - col).astype(F32)
    idx = lax.broadcasted_iota(jnp.int32, (CHUNK, 1), 0).astype(F32)

    outs = []
    for h in range(RET_HEADS):
        lgam = _ret_log_gamma(h)
        q = q_all[:, h * RET_DK:(h + 1) * RET_DK]
        k = k_all[:, h * RET_DK:(h + 1) * RET_DK]
        sl = slice(h * RET_DV, (h + 1) * RET_DV)
        v = v_all[:, sl]
        d_intra = jnp.where(delta >= 0, jnp.exp(lgam * jnp.maximum(delta, 0.0)), 0.0)
        intra = _bdot(_bdot_nt(q, k) * d_intra, v)
        q_dec = q * jnp.exp(lgam * (idx + 1.0))
        k_dec = k * jnp.exp(lgam * (CHUNK - 1.0 - idx))
        s0 = state[h]
        y = intra + _bdot(q_dec, s0)
        state[h] = s0 * math.exp(lgam * CHUNK) + _bdot_tn(k_dec, v)
        mu = jnp.mean(y, -1, keepdims=True)
        d = y - mu
        var = jnp.mean(d * d, -1, keepdims=True)
        yn = d * lax.rsqrt(var + LN_EPS) * ng_ref[:, sl] + nb_ref[:, sl]
        outs.append(yn * _silu(gate_all[:, sl]))
    o_ref[0] = jnp.concatenate(outs, axis=1).astype(o_ref.dtype)


def _ret_mixer(pc, cos_t, sin_t, norm_g, norm_b):
    bsz, seq, _ = pc.shape
    qk_w = RET_HEADS * RET_DK
    return pl.pallas_call(
        _ret_kernel,
        grid=(bsz, seq // CHUNK),
        in_specs=[pl.BlockSpec((1, CHUNK, D_C_IN), lambda b, c: (b, c, 0)),
                  pl.BlockSpec((CHUNK, qk_w), lambda b, c: (c, 0)),
                  pl.BlockSpec((CHUNK, qk_w), lambda b, c: (c, 0)),
                  _const_spec((1, D_C)), _const_spec((1, D_C))],
        out_specs=pl.BlockSpec((1, CHUNK, D_C), lambda b, c: (b, c, 0)),
        out_shape=jax.ShapeDtypeStruct((bsz, seq, D_C), BF16),
        scratch_shapes=[pltpu.VMEM((RET_HEADS, RET_DK, RET_DV), F32)],
        compiler_params=pltpu.CompilerParams(dimension_semantics=("parallel", "arbitrary"),
                                             vmem_limit_bytes=VMEM_LIMIT_BYTES),
        name="ret_mixer",
    )(pc, cos_t, sin_t, norm_g, norm_b)


def _rope_tables(seq):
    half = RET_DK // 2
    inv_freq = 1.0 / (ROPE_BASE ** jnp.linspace(0.0, 1.0, half, dtype=F32))
    ang = jnp.arange(seq, dtype=F32)[:, None] * inv_freq
    cos, sin = jnp.cos(ang), jnp.sin(ang)
    cos_t = jnp.tile(jnp.concatenate([cos, cos], -1), (1, RET_HEADS))
    sin_t = jnp.tile(jnp.concatenate([-sin, sin], -1), (1, RET_HEADS))
    return cos_t, sin_t


def _row(v):
    return v.reshape(1, -1)


def _pad_gate_vec(v):
    return jnp.zeros((1, GATE_PAD), F32).at[0, GDN_HEADS:2 * GDN_HEADS].set(v)


def _mixer_layer(h3, l, w_in, w_out, ln_g, ln_b, gdn, rwkv, ret, rope):
    bsz, seq, _ = h3.shape
    wa = jnp.concatenate([w_in[:, :GDN_QKV + D_A + 2 * GDN_HEADS],
                          jnp.zeros((D_MODEL, GATE_PAD - 2 * GDN_HEADS), w_in.dtype)], axis=1)
    wb = w_in[:, D_A_IN:D_A_IN + D_B_IN]
    wc = w_in[:, D_A_IN + D_B_IN:]
    pa, pb, pc = _in_proj(h3, wa.astype(BF16), wb.astype(BF16), wc.astype(BF16))
    conv_w, a_log, dt_bias, norm_w = gdn
    ya = _gdn_mixer(pa, conv_w, _pad_gate_vec(a_log), _pad_gate_vec(dt_bias), _row(norm_w))
    yb = _rwkv_mixer(pb, *[_row(p) if p.ndim == 1 else p for p in rwkv])
    yc = _ret_mixer(pc, rope[0], rope[1], _row(ret[0]), _row(ret[1]))
    rows = bsz * seq
    out = _out_proj_ln(h3.reshape(rows, D_MODEL), ya.reshape(rows, D_A), yb.reshape(rows, D_B),
                       yc.reshape(rows, D_C), w_out.astype(BF16), _row(ln_g), _row(ln_b))
    return out.reshape(bsz, seq, D_MODEL)


def kernel(x, meta_tokens, ln_g, ln_b, w_ff1_in, w_ff1_out, w_ff2_in, w_ff2_out, w_in, w_out,
           gdn_conv_w, gdn_a_log, gdn_dt_bias, gdn_norm_w, rwkv_mu, rwkv_w0, rwkv_w_up, rwkv_a0,
           rwkv_a_up, rwkv_g_up, rwkv_k_k, rwkv_k_a, rwkv_r_k, rwkv_lnx_g, rwkv_lnx_b,
           ret_norm_g, ret_norm_b):
    bsz, _, d = x.shape
    pad = jnp.zeros((bsz, META_PAD, d), x.dtype)
    meta = jnp.broadcast_to(meta_tokens.astype(x.dtype)[None], (bsz, N_META, d))
    h = jnp.concatenate([pad, meta, x], axis=1)
    seq = h.shape[1]
    rows = bsz * seq
    rope = _rope_tables(seq)
    for l in range(ln_g.shape[0]):
        h = _ffn_ln(h.reshape(rows, d), w_ff1_in[l].astype(BF16), w_ff1_out[l].astype(BF16),
                    _row(ln_g[l, 0]), _row(ln_b[l, 0])).reshape(bsz, seq, d)
        h = _mixer_layer(
            h, l, w_in[l], w_out[l], ln_g[l, 1], ln_b[l, 1],
            (gdn_conv_w[l], gdn_a_log[l], gdn_dt_bias[l], gdn_norm_w[l]),
            (rwkv_mu[l], rwkv_w0[l], rwkv_w_up[l], rwkv_a0[l], rwkv_a_up[l], rwkv_g_up[l],
             rwkv_k_k[l], rwkv_k_a[l], rwkv_r_k[l].reshape(-1), rwkv_lnx_g[l], rwkv_lnx_b[l]),
            (ret_norm_g[l], ret_norm_b[l]), rope)
        h = _ffn_ln(h.reshape(rows, d), w_ff2_in[l].astype(BF16), w_ff2_out[l].astype(BF16),
                    _row(ln_g[l, 2]), _row(ln_b[l, 2])).reshape(bsz, seq, d)
    return h[:, CHUNK:]
```

```python
import functools
import math

import jax
import jax.numpy as jnp
from jax import lax
from jax.experimental import pallas as pl
from jax.experimental.pallas import tpu as pltpu

F32 = jnp.float32
BF16 = jnp.bfloat16

D_MODEL = 1024
DEPTH = 2
CHUNK = 64
N_META = 16
META_PAD = CHUNK - N_META
D_FF = 2816
LN_EPS = 1e-5
ALPHA = (2.0 * DEPTH) ** 0.25

GDN_HEADS, GDN_DK, GDN_DV, GDN_CONV = 4, 128, 128, 4
GDN_QKV = 2 * GDN_HEADS * GDN_DK + GDN_HEADS * GDN_DV
D_A = GDN_HEADS * GDN_DV
GATE_PAD = 128
D_A_COLS = GDN_QKV + D_A + GATE_PAD
RWKV_HEADS, RWKV_HEAD = 4, 64
RWKV_DECAY_RANK, RWKV_A_RANK, RWKV_GATE_RANK = 32, 32, 64
RWKV_LNX_EPS = 64e-5
D_B = RWKV_HEADS * RWKV_HEAD
D_B_IN = 3 * D_B + RWKV_DECAY_RANK + RWKV_A_RANK + RWKV_GATE_RANK
RET_HEADS, RET_DK, RET_DV = 4, 32, 64
ROPE_BASE = 10000.0
D_C = RET_HEADS * RET_DV
D_C_IN = 2 * RET_HEADS * RET_DK + 2 * D_C
D_A_IN = GDN_QKV + D_A + 2 * GDN_HEADS

VMEM_LIMIT_BYTES = 56 * 1024 * 1024
ROW_TILE = 768
FF_CHUNKS = ((0, 1024), (1024, 1024), (2048, 768))


def _dot(a, b):
    return jnp.dot(a, b, preferred_element_type=F32)


def _bdot(a, b):
    return _dot(a.astype(BF16), b.astype(BF16))


def _layer_norm(y, g, b):
    mu = jnp.mean(y, axis=-1, keepdims=True)
    d = y - mu
    var = jnp.mean(d * d, axis=-1, keepdims=True)
    return d * lax.rsqrt(var + LN_EPS) * g + b


def _silu(x):
    return x * jax.nn.sigmoid(x)


def _softplus(x):
    return jnp.maximum(x, 0.0) + jnp.log(1.0 + jnp.exp(-jnp.abs(x)))


def _tri_masks():
    row = lax.broadcasted_iota(jnp.int32, (CHUNK, CHUNK), 0)
    col = lax.broadcasted_iota(jnp.int32, (CHUNK, CHUNK), 1)
    return row >= col, row > col, row == col


def _split2(x):
    hi = x.astype(BF16)
    lo = (x - hi.astype(F32)).astype(BF16)
    return hi, lo


def _split3(x):
    hi = x.astype(BF16)
    r1 = x - hi.astype(F32)
    mid = r1.astype(BF16)
    lo = (r1 - mid.astype(F32)).astype(BF16)
    return hi, mid, lo


def _mm3(a, b):
    ah, al = _split2(a)
    bh, bl = _split2(b)
    return _dot(ah, bh) + (_dot(ah, bl) + _dot(al, bh))


def _mm_exact_lhs(a_bf16, pieces):
    out = _dot(a_bf16, pieces[0])
    for p in pieces[1:]:
        out = out + _dot(a_bf16, p)
    return out


def _mm_exact_rhs(pieces, b_bf16):
    out = _dot(pieces[0], b_bf16)
    for p in pieces[1:]:
        out = out + _dot(p, b_bf16)
    return out


def _ffn_ln_kernel(x_ref, win_ref, wout_ref, g_ref, b_ref, o_ref):
    x = x_ref[...]
    xb = x.astype(BF16)
    acc = jnp.zeros(x.shape, F32)
    for start, width in FF_CHUNKS:
        gate = _dot(xb, win_ref[:, start:start + width])
        up = _dot(xb, win_ref[:, D_FF + start:D_FF + start + width])
        act = (_silu(gate) * up).astype(BF16)
        acc = acc + _dot(act, wout_ref[start:start + width, :])
    o_ref[...] = _layer_norm(ALPHA * x + 0.5 * acc, g_ref[...], b_ref[...])


def _const_spec(shape):
    return pl.BlockSpec(shape, lambda *_: (0,) * len(shape), pipeline_mode=pl.Buffered(1))


def _ffn_ln(h, w_in, w_out, g, b):
    rows = h.shape[0]
    tile = ROW_TILE if rows % ROW_TILE == 0 else rows
    return pl.pallas_call(
        _ffn_ln_kernel,
        grid=(rows // tile,),
        in_specs=[pl.BlockSpec((tile, D_MODEL), lambda i: (i, 0)),
                  _const_spec((D_MODEL, 2 * D_FF)),
                  _const_spec((D_FF, D_MODEL)),
                  _const_spec((1, D_MODEL)),
                  _const_spec((1, D_MODEL))],
        out_specs=pl.BlockSpec((tile, D_MODEL), lambda i: (i, 0)),
        out_shape=jax.ShapeDtypeStruct((rows, D_MODEL), F32),
        compiler_params=pltpu.CompilerParams(dimension_semantics=("parallel",),
                                             vmem_limit_bytes=VMEM_LIMIT_BYTES),
        name="ffn_ln",
    )(h, w_in, w_out, g, b)


def _in_proj_kernel(h_ref, wa_ref, wb_ref, wc_ref, pa_ref, pb_ref, pc_ref, *, tile):
    pos = pl.program_id(1) * tile + lax.broadcasted_iota(jnp.int32, (tile, 1), 0)
    hb = jnp.where(pos >= META_PAD, h_ref[0], 0.0).astype(BF16)
    pa_ref[0] = _dot(hb, wa_ref[...])
    pb_ref[0] = _dot(hb, wb_ref[...])
    pc_ref[0] = _dot(hb, wc_ref[...])


def _seq_tile(seq):
    for cand in (1032, 688, 344, CHUNK):
        if seq % cand == 0:
            return cand
    return seq


def _in_proj(h, wa, wb, wc):
    bsz, seq, _ = h.shape
    tile = _seq_tile(seq)
    row = lambda width: pl.BlockSpec((1, tile, width), lambda b, j: (b, j, 0))
    return pl.pallas_call(
        functools.partial(_in_proj_kernel, tile=tile),
        grid=(bsz, seq // tile),
        in_specs=[row(D_MODEL), _const_spec((D_MODEL, D_A_COLS)), _const_spec((D_MODEL, D_B_IN)),
                  _const_spec((D_MODEL, D_C_IN))],
        out_specs=[row(D_A_COLS), row(D_B_IN), row(D_C_IN)],
        out_shape=[jax.ShapeDtypeStruct((bsz, seq, D_A_COLS), F32),
                   jax.ShapeDtypeStruct((bsz, seq, D_B_IN), F32),
                   jax.ShapeDtypeStruct((bsz, seq, D_C_IN), F32)],
        compiler_params=pltpu.CompilerParams(dimension_semantics=("parallel", "parallel"),
                                             vmem_limit_bytes=VMEM_LIMIT_BYTES),
        name="in_proj",
    )(h, wa, wb, wc)


def _out_proj_ln_kernel(h_ref, ya_ref, yb_ref, yc_ref, w_ref, g_ref, b_ref, o_ref):
    mix = _dot(ya_ref[...], w_ref[0:D_A, :])
    mix = mix + _dot(yb_ref[...], w_ref[D_A:D_A + D_B, :])
    mix = mix + _dot(yc_ref[...], w_ref[D_A + D_B:, :])
    o_ref[...] = _layer_norm(ALPHA * h_ref[...] + mix, g_ref[...], b_ref[...])


def _out_proj_ln(h, ya, yb, yc, w_out, g, b):
    rows = h.shape[0]
    tile = ROW_TILE if rows % ROW_TILE == 0 else rows
    row = lambda width: pl.BlockSpec((tile, width), lambda i: (i, 0))
    return pl.pallas_call(
        _out_proj_ln_kernel,
        grid=(rows // tile,),
        in_specs=[row(D_MODEL), row(D_A), row(D_B), row(D_C),
                  _const_spec((D_A + D_B + D_C, D_MODEL)),
                  _const_spec((1, D_MODEL)), _const_spec((1, D_MODEL))],
        out_specs=row(D_MODEL),
        out_shape=jax.ShapeDtypeStruct((rows, D_MODEL), F32),
        compiler_params=pltpu.CompilerParams(dimension_semantics=("parallel",),
                                             vmem_limit_bytes=VMEM_LIMIT_BYTES),
        name="out_proj_ln",
    )(h, ya, yb, yc, w_out, g, b)


def _bmm(a, b):
    return jnp.einsum('nij,njk->nik', a, b, preferred_element_type=F32)


def _bmm_nt(a, b):
    return jnp.einsum('nik,njk->nij', a, b, preferred_element_type=F32)


def _bmm_tn(a, b):
    return jnp.einsum('nki,nkj->nij', a, b, preferred_element_type=F32)


def _b1(mm, a, b):
    return mm(a.astype(BF16), b.astype(BF16))


def _b3(mm, a, b):
    ah, al = _split2(a)
    bh, bl = _split2(b)
    return mm(ah, bh) + (mm(ah, bl) + mm(al, bh))


def _chain_series(x, eye):
    p = eye + x
    for _ in range(5):
        xh, xl = _split2(x)
        x = _bmm(xh, xh) + (_bmm(xh, xl) + _bmm(xl, xh))
        ph, pl_ = _split2(p)
        xh, xl = _split2(x)
        p = p + (_bmm(ph, xh) + (_bmm(ph, xl) + _bmm(pl_, xh)))
    return p


def _to_chains(x3, heads, width, offset=0):
    return jnp.concatenate([x3[:, :, offset + h * width:offset + (h + 1) * width] for h in range(heads)], axis=0)


def _from_chains(xn, heads):
    bsz = xn.shape[0] // heads
    return jnp.concatenate([xn[h * bsz:(h + 1) * bsz] for h in range(heads)], axis=2)


def _block_tril(rows):
    r = lax.broadcasted_iota(jnp.int32, (rows, rows), 0)
    c = lax.broadcasted_iota(jnp.int32, (rows, rows), 1)
    return ((r // CHUNK == c // CHUNK) & (r >= c)).astype(BF16)


def _head_group_matrix(width, head, value):
    r = lax.broadcasted_iota(jnp.int32, (width, width), 0) // head
    c = lax.broadcasted_iota(jnp.int32, (width, width), 1) // head
    return jnp.where(r == c, value, 0.0).astype(BF16)


def _gdn_kernel(pa_ref, conv_ref, alog_ref, dt_ref, nw_ref, o_ref, cbuf, state):
    bsz = pa_ref.shape[0]
    rows = bsz * CHUNK

    @pl.when(pl.program_id(0) == 0)
    def _():
        cbuf[:, 0:8, :] = jnp.zeros((bsz, 8, GDN_QKV), F32)
        state[...] = jnp.zeros(state.shape, F32)

    causal, strict, diag = _tri_masks()
    eye = diag.astype(F32)

    x = pa_ref[:, :, 0:GDN_QKV]
    cbuf[:, 8:8 + CHUNK, :] = x
    y = conv_ref[3:4, :] * x
    for tap in range(GDN_CONV - 1):
        lo = 8 - (GDN_CONV - 1) + tap
        y = y + conv_ref[tap:tap + 1, :] * cbuf[:, lo:lo + CHUNK, :]
    cbuf[:, 0:8, :] = x[:, CHUNK - 8:, :]
    qkv = _silu(y)

    gates = pa_ref[:, :, GDN_QKV + D_A:].reshape(rows, GATE_PAD)
    beta_all = jax.nn.sigmoid(gates).reshape(bsz, CHUNK, GATE_PAD)
    lg_all = -jnp.exp(alog_ref[...]) * _softplus(gates + dt_ref[...])
    g_all = _mm_exact_lhs(_block_tril(rows), _split3(lg_all))
    g_t = g_all.T
    g_all = g_all.reshape(bsz, CHUNK, GATE_PAD)
    beta = jnp.concatenate([beta_all[:, :, h:h + 1] for h in range(GDN_HEADS)], axis=0)
    gc = jnp.concatenate([g_all[:, :, GDN_HEADS + h:GDN_HEADS + h + 1] for h in range(GDN_HEADS)], axis=0)
    g_row = jnp.concatenate([g_t[GDN_HEADS + h:GDN_HEADS + h + 1, b * CHUNK:(b + 1) * CHUNK][None]
                             for h in range(GDN_HEADS) for b in range(bsz)], axis=0)

    q = _to_chains(qkv, GDN_HEADS, GDN_DK)
    k = _to_chains(qkv, GDN_HEADS, GDN_DK, GDN_HEADS * GDN_DK)
    v = _to_chains(qkv, GDN_HEADS, GDN_DV, 2 * GDN_HEADS * GDN_DK)
    q = q * (lax.rsqrt(jnp.sum(q * q, -1, keepdims=True) + 1e-6) * GDN_DK ** -0.5)
    k = k * lax.rsqrt(jnp.sum(k * k, -1, keepdims=True) + 1e-6)
    diff = gc - g_row
    decay = jnp.where(causal, jnp.exp(jnp.where(causal, diff, 0.0)), 0.0)
    kb = k * beta
    kq = _b1(_bmm_nt, jnp.concatenate([kb, q], axis=1), k)
    a_mat = jnp.where(strict, kq[:, :CHUNK] * decay, 0.0)
    qk = kq[:, CHUNK:] * decay
    t_inv = _chain_series(-a_mat, eye)
    e_g = jnp.exp(gc)
    sol = _b3(_bmm, t_inv, jnp.concatenate([v * beta, kb * e_g], axis=2))
    u, w = sol[:, :, :GDN_DV], sol[:, :, GDN_DV:]
    g_last = gc[:, CHUNK - 1:CHUNK, :]
    k_dec = k * jnp.exp(g_last - gc)
    s0 = state[...]
    ws = _b1(_bmm, jnp.concatenate([w, q * e_g], axis=1), s0)
    v_new = u - ws[:, :CHUNK]
    o = ws[:, CHUNK:] + _b1(_bmm, qk, v_new)
    state[...] = s0 * jnp.exp(g_last) + _b1(_bmm_tn, k_dec, v_new)
    o = o * lax.rsqrt(jnp.mean(o * o, -1, keepdims=True) + LN_EPS) * nw_ref[...]
    z = _to_chains(pa_ref[:, :, GDN_QKV:GDN_QKV + D_A], GDN_HEADS, GDN_DV)
    o_ref[...] = _from_chains(o * _silu(z), GDN_HEADS).astype(o_ref.dtype)


def _gdn_mixer(pa, conv_w, a_log_pad, dt_pad, norm_w):
    bsz, seq, _ = pa.shape
    return pl.pallas_call(
        _gdn_kernel,
        grid=(seq // CHUNK,),
        in_specs=[pl.BlockSpec((bsz, CHUNK, D_A_COLS), lambda c: (0, c, 0)),
                  _const_spec((GDN_CONV, GDN_QKV)), _const_spec((1, GATE_PAD)),
                  _const_spec((1, GATE_PAD)), _const_spec((1, GDN_DV))],
        out_specs=pl.BlockSpec((bsz, CHUNK, D_A), lambda c: (0, c, 0)),
        out_shape=jax.ShapeDtypeStruct((bsz, seq, D_A), BF16),
        scratch_shapes=[pltpu.VMEM((bsz, 8 + CHUNK, GDN_QKV), F32),
                        pltpu.VMEM((GDN_HEADS * bsz, GDN_DK, GDN_DV), F32)],
        compiler_params=pltpu.CompilerParams(dimension_semantics=("arbitrary",),
                                             vmem_limit_bytes=VMEM_LIMIT_BYTES),
        name="gdn_mixer",
    )(pa, conv_w, a_log_pad, dt_pad, norm_w)


def _rwkv_kernel(pb_ref, mu_ref, w0_ref, wup_ref, a0_ref, aup_ref, gup_ref, kk_ref, ka_ref,
                 rk_ref, lg_ref, lb_ref, o_ref, sbuf, state):
    bsz = pb_ref.shape[0]
    rows = bsz * CHUNK

    @pl.when(pl.program_id(0) == 0)
    def _():
        sbuf[:, 0:8, :] = jnp.zeros((bsz, 8, D_B_IN), F32)
        state[...] = jnp.zeros(state.shape, F32)

    causal, strict, diag = _tri_masks()
    eye = diag.astype(F32)
    head_ones = _head_group_matrix(D_B, RWKV_HEAD, 1.0)
    head_avg = _head_group_matrix(D_B, RWKV_HEAD, 1.0 / RWKV_HEAD)

    x = pb_ref[...]
    sbuf[:, 8:8 + CHUNK, :] = x
    prev = sbuf[:, 7:7 + CHUNK, :]
    sbuf[:, 0:8, :] = x[:, CHUNK - 8:, :]
    pf = (x + (prev - x) * mu_ref[...]).reshape(rows, D_B_IN)
    r_all, k_all, v_all = pf[:, 0:D_B], pf[:, D_B:2 * D_B], pf[:, 2 * D_B:3 * D_B]
    c0 = 3 * D_B
    xw = pf[:, c0:c0 + RWKV_DECAY_RANK]
    xa = pf[:, c0 + RWKV_DECAY_RANK:c0 + RWKV_DECAY_RANK + RWKV_A_RANK]
    xg = pf[:, c0 + RWKV_DECAY_RANK + RWKV_A_RANK:]
    log_w = -_softplus(-(w0_ref[...] + _mm3(jnp.tanh(xw), wup_ref[...]))) - 0.5
    wlog = -jnp.exp(log_w)
    a_all = jax.nn.sigmoid(a0_ref[...] + _mm3(xa, aup_ref[...]))
    gate_all = _bdot(jax.nn.sigmoid(xg), gup_ref[...])
    kk = k_all * kk_ref[...]
    kk = kk * lax.rsqrt(_mm_exact_rhs(_split2(kk * kk), head_ones) + 1e-6)
    k2 = k_all * (1.0 + (a_all - 1.0) * ka_ref[...])
    bb = kk * a_all
    gc = _mm_exact_lhs(_block_tril(rows), _split3(wlog))
    e_neg = jnp.exp(-gc)
    r_t = r_all * jnp.exp(gc)
    a_t = -kk * jnp.exp(gc - wlog)
    k_t = k2 * e_neg
    b_t = bb * e_neg
    gc3 = gc.reshape(bsz, CHUNK, D_B)
    g_last = gc3[:, CHUNK - 1:CHUNK, :]
    e_tail = jnp.exp(g_last - gc3).reshape(rows, D_B)
    bonus = _mm_exact_rhs(_split2(r_all * k2 * rk_ref[...]), head_ones) * v_all

    chains = lambda t: _to_chains(t.reshape(bsz, CHUNK, D_B), RWKV_HEADS, RWKV_HEAD)
    a_n, r_n, b_n, k_n, v_n = chains(a_t), chains(r_t), chains(b_t), chains(k_t), chains(v_all)
    bdec_n, kdec_n = chains(bb * e_tail), chains(k2 * e_tail)
    elast_n = _to_chains(jnp.exp(g_last), RWKV_HEADS, RWKV_HEAD)

    ar = jnp.concatenate([a_n, r_n], axis=1)
    xb = _b3(_bmm_nt, ar, b_n)
    xk = _b3(_bmm_nt, ar, k_n)
    a_ab = jnp.where(strict, xb[:, :CHUNK], 0.0)
    a_rb = jnp.where(causal, xb[:, CHUNK:], 0.0)
    a_k = jnp.concatenate([jnp.where(strict, xk[:, :CHUNK], 0.0), jnp.where(causal, xk[:, CHUNK:], 0.0)], axis=1)
    m_inv = _chain_series(a_ab, eye)
    akv = _b1(_bmm, a_k, v_n)
    w_mat = _b3(_bmm, m_inv, a_n)
    u2 = _b3(_bmm, m_inv, akv[:, :CHUNK])
    s0 = state[...]
    wr = _b1(_bmm_nt, jnp.concatenate([w_mat, r_n], axis=1), s0)
    u = wr[:, :CHUNK] + u2
    y_n = wr[:, CHUNK:] + _b1(_bmm, a_rb, u) + akv[:, CHUNK:]
    state[...] = s0 * elast_n + _b1(_bmm_tn, jnp.concatenate([u, v_n], axis=1),
                                    jnp.concatenate([bdec_n, kdec_n], axis=1))

    y = _from_chains(y_n, RWKV_HEADS).reshape(rows, D_B)
    mu = _mm_exact_rhs(_split2(y), head_avg)
    d = y - mu
    var = _mm_exact_rhs(_split2(d * d), head_avg)
    yn = d * lax.rsqrt(var + RWKV_LNX_EPS) * lg_ref[...] + lb_ref[...]
    o_ref[...] = ((yn + bonus) * gate_all).reshape(bsz, CHUNK, D_B).astype(o_ref.dtype)


def _rwkv_mixer(pb, mu, w0, w_up, a0, a_up, g_up, k_k, k_a, r_k, lnx_g, lnx_b):
    bsz, seq, _ = pb.shape
    vec = _const_spec((1, D_B))
    return pl.pallas_call(
        _rwkv_kernel,
        grid=(seq // CHUNK,),
        in_specs=[pl.BlockSpec((bsz, CHUNK, D_B_IN), lambda c: (0, c, 0)),
                  _const_spec((1, D_B_IN)), vec, _const_spec((RWKV_DECAY_RANK, D_B)), vec,
                  _const_spec((RWKV_A_RANK, D_B)), _const_spec((RWKV_GATE_RANK, D_B)),
                  vec, vec, vec, vec, vec],
        out_specs=pl.BlockSpec((bsz, CHUNK, D_B), lambda c: (0, c, 0)),
        out_shape=jax.ShapeDtypeStruct((bsz, seq, D_B), BF16),
        scratch_shapes=[pltpu.VMEM((bsz, 8 + CHUNK, D_B_IN), F32),
                        pltpu.VMEM((RWKV_HEADS * bsz, RWKV_HEAD, RWKV_HEAD), F32)],
        compiler_params=pltpu.CompilerParams(dimension_semantics=("arbitrary",),
                                             vmem_limit_bytes=VMEM_LIMIT_BYTES),
        name="rwkv_mixer",
    )(pb, mu, w0, w_up, a0, a_up, g_up, k_k, k_a, r_k, lnx_g, lnx_b)


def _ret_log_gamma(h):
    return math.log(1.0 - 2.0 ** (-5.0 - h))


def _ret_kernel(pc_ref, cos_ref, sin_ref, ng_ref, nb_ref, o_ref, state):
    bsz = pc_ref.shape[0]
    rows = bsz * CHUNK

    @pl.when(pl.program_id(0) == 0)
    def _():
        state[...] = jnp.zeros(state.shape, F32)

    qk_w = RET_HEADS * RET_DK
    lane = lax.broadcasted_iota(jnp.int32, (bsz, CHUNK, qk_w), 2)
    first_half = (lane % RET_DK) < (RET_DK // 2)

    def rope(t):
        partner = jnp.where(first_half, pltpu.roll(t, qk_w - RET_DK // 2, 2), pltpu.roll(t, RET_DK // 2, 2))
        return t * cos_ref[...] + partner * sin_ref[...]

    row = lax.broadcasted_iota(jnp.int32, (CHUNK, CHUNK), 0)
    col = lax.broadcasted_iota(jnp.int32, (CHUNK, CHUNK), 1)
    delta = (row - col).astype(F32)
    idx = lax.broadcasted_iota(jnp.int32, (CHUNK, 1), 0).astype(F32)
    head_avg = _head_group_matrix(D_C, RET_DV, 1.0 / RET_DV)
    rep = lambda t: jnp.broadcast_to(t[None], (bsz,) + t.shape)
    d_intra = jnp.concatenate(
        [rep(jnp.where(delta >= 0, jnp.exp(_ret_log_gamma(h) * jnp.maximum(delta, 0.0)), 0.0))
         for h in range(RET_HEADS)], axis=0)
    q_scale = jnp.concatenate([rep(jnp.exp(_ret_log_gamma(h) * (idx + 1.0))) for h in range(RET_HEADS)], axis=0)
    k_scale = jnp.concatenate([rep(jnp.exp(_ret_log_gamma(h) * (CHUNK - 1.0 - idx))) for h in range(RET_HEADS)],
                              axis=0)
    s_scale = jnp.concatenate([jnp.full((bsz, 1, 1), math.exp(_ret_log_gamma(h) * CHUNK), F32)
                               for h in range(RET_HEADS)], axis=0)

    q = _to_chains(rope(pc_ref[:, :, 0:qk_w]), RET_HEADS, RET_DK)
    k = _to_chains(rope(pc_ref[:, :, qk_w:2 * qk_w]) * RET_DK ** -0.5, RET_HEADS, RET_DK)
    v = _to_chains(pc_ref[:, :, 2 * qk_w:2 * qk_w + D_C], RET_HEADS, RET_DV)
    intra = _b1(_bmm, _b1(_bmm_nt, q, k) * d_intra, v)
    s0 = state[...]
    y_n = intra + _b1(_bmm, q * q_scale, s0)
    state[...] = s0 * s_scale + _b1(_bmm_tn, k * k_scale, v)
    y = _from_chains(y_n, RET_HEADS).reshape(rows, D_C)
    mu = _mm_exact_rhs(_split2(y), head_avg)
    d = y - mu
    var = _mm_exact_rhs(_split2(d * d), head_avg)
    yn = d * lax.rsqrt(var + LN_EPS) * ng_ref[...] + nb_ref[...]
    gate = pc_ref[:, :, 2 * qk_w + D_C:].reshape(rows, D_C)
    o_ref[...] = (yn * _silu(gate)).reshape(bsz, CHUNK, D_C).astype(o_ref.dtype)


def _ret_mixer(pc, cos_t, sin_t, norm_g, norm_b):
    bsz, seq, _ = pc.shape
    qk_w = RET_HEADS * RET_DK
    return pl.pallas_call(
        _ret_kernel,
        grid=(seq // CHUNK,),
        in_specs=[pl.BlockSpec((bsz, CHUNK, D_C_IN), lambda c: (0, c, 0)),
                  pl.BlockSpec((CHUNK, qk_w), lambda c: (c, 0)),
                  pl.BlockSpec((CHUNK, qk_w), lambda c: (c, 0)),
                  _const_spec((1, D_C)), _const_spec((1, D_C))],
        out_specs=pl.BlockSpec((bsz, CHUNK, D_C), lambda c: (0, c, 0)),
        out_shape=jax.ShapeDtypeStruct((bsz, seq, D_C), BF16),
        scratch_shapes=[pltpu.VMEM((RET_HEADS * bsz, RET_DK, RET_DV), F32)],
        compiler_params=pltpu.CompilerParams(dimension_semantics=("arbitrary",),
                                             vmem_limit_bytes=VMEM_LIMIT_BYTES),
        name="ret_mixer",
    )(pc, cos_t, sin_t, norm_g, norm_b)


def _rope_tables(seq):
    half = RET_DK // 2
    inv_freq = 1.0 / (ROPE_BASE ** jnp.linspace(0.0, 1.0, half, dtype=F32))
    ang = jnp.arange(seq, dtype=F32)[:, None] * inv_freq
    cos, sin = jnp.cos(ang), jnp.sin(ang)
    cos_t = jnp.tile(jnp.concatenate([cos, cos], -1), (1, RET_HEADS))
    sin_t = jnp.tile(jnp.concatenate([-sin, sin], -1), (1, RET_HEADS))
    return cos_t, sin_t


def _row(v):
    return v.reshape(1, -1)


def _pad_gate_vec(v):
    return jnp.zeros((1, GATE_PAD), F32).at[0, GDN_HEADS:2 * GDN_HEADS].set(v)


def _mixer_layer(h3, w_in, w_out, ln_g, ln_b, gdn, rwkv, ret, rope):
    bsz, seq, _ = h3.shape
    wa = jnp.concatenate([w_in[:, :D_A_IN], jnp.zeros((D_MODEL, D_A_COLS - D_A_IN), w_in.dtype)], axis=1)
    wb = w_in[:, D_A_IN:D_A_IN + D_B_IN]
    wc = w_in[:, D_A_IN + D_B_IN:]
    pa, pb, pc = _in_proj(h3, wa.astype(BF16), wb.astype(BF16), wc.astype(BF16))
    conv_w, a_log, dt_bias, norm_w = gdn
    ya = _gdn_mixer(pa, conv_w, _pad_gate_vec(a_log), _pad_gate_vec(dt_bias), _row(norm_w))
    yb = _rwkv_mixer(pb, *[_row(p) if p.ndim == 1 else p for p in rwkv])
    yc = _ret_mixer(pc, rope[0], rope[1], _row(ret[0]), _row(ret[1]))
    rows = bsz * seq
    out = _out_proj_ln(h3.reshape(rows, D_MODEL), ya.reshape(rows, D_A), yb.reshape(rows, D_B),
                       yc.reshape(rows, D_C), w_out.astype(BF16), _row(ln_g), _row(ln_b))
    return out.reshape(bsz, seq, D_MODEL)


def kernel(x, meta_tokens, ln_g, ln_b, w_ff1_in, w_ff1_out, w_ff2_in, w_ff2_out, w_in, w_out,
           gdn_conv_w, gdn_a_log, gdn_dt_bias, gdn_norm_w, rwkv_mu, rwkv_w0, rwkv_w_up, rwkv_a0,
           rwkv_a_up, rwkv_g_up, rwkv_k_k, rwkv_k_a, rwkv_r_k, rwkv_lnx_g, rwkv_lnx_b,
           ret_norm_g, ret_norm_b):
    bsz, _, d = x.shape
    pad = jnp.zeros((bsz, META_PAD, d), x.dtype)
    meta = jnp.broadcast_to(meta_tokens.astype(x.dtype)[None], (bsz, N_META, d))
    h = jnp.concatenate([pad, meta, x], axis=1)
    seq = h.shape[1]
    rows = bsz * seq
    rope = _rope_tables(seq)
    for l in range(ln_g.shape[0]):
        h = _ffn_ln(h.reshape(rows, d), w_ff1_in[l].astype(BF16), w_ff1_out[l].astype(BF16),
                    _row(ln_g[l, 0]), _row(ln_b[l, 0])).reshape(bsz, seq, d)
        h = _mixer_layer(
            h, w_in[l], w_out[l], ln_g[l, 1], ln_b[l, 1],
            (gdn_conv_w[l], gdn_a_log[l], gdn_dt_bias[l], gdn_norm_w[l]),
            (rwkv_mu[l], rwkv_w0[l], rwkv_w_up[l], rwkv_a0[l], rwkv_a_up[l], rwkv_g_up[l],
             rwkv_k_k[l], rwkv_k_a[l], rwkv_r_k[l].reshape(-1), rwkv_lnx_g[l], rwkv_lnx_b[l]),
            (ret_norm_g[l], ret_norm_b[l]), rope)
        h = _ffn_ln(h.reshape(rows, d), w_ff2_in[l].astype(BF16), w_ff2_out[l].astype(BF16),
                    _row(ln_g[l, 2]), _row(ln_b[l, 2])).reshape(bsz, seq, d)
    return h[:, CHUNK:]
```

```python
import functools
import math

import jax
import jax.numpy as jnp
from jax import lax
from jax.experimental import pallas as pl
from jax.experimental.pallas import tpu as pltpu

F32 = jnp.float32
BF16 = jnp.bfloat16

D_MODEL = 1024
DEPTH = 2
CHUNK = 64
N_META = 16
META_PAD = CHUNK - N_META
D_FF = 2816
LN_EPS = 1e-5
ALPHA = (2.0 * DEPTH) ** 0.25

GDN_HEADS, GDN_DK, GDN_DV, GDN_CONV = 4, 128, 128, 4
GDN_QKV = 2 * GDN_HEADS * GDN_DK + GDN_HEADS * GDN_DV
D_A = GDN_HEADS * GDN_DV
GATE_PAD = 128
D_A_COLS = GDN_QKV + D_A + GATE_PAD
RWKV_HEADS, RWKV_HEAD = 4, 64
RWKV_DECAY_RANK, RWKV_A_RANK, RWKV_GATE_RANK = 32, 32, 64
RWKV_LNX_EPS = 64e-5
D_B = RWKV_HEADS * RWKV_HEAD
D_B_IN = 3 * D_B + RWKV_DECAY_RANK + RWKV_A_RANK + RWKV_GATE_RANK
RET_HEADS, RET_DK, RET_DV = 4, 32, 64
ROPE_BASE = 10000.0
D_C = RET_HEADS * RET_DV
D_C_IN = 2 * RET_HEADS * RET_DK + 2 * D_C
D_A_IN = GDN_QKV + D_A + 2 * GDN_HEADS

VMEM_LIMIT_BYTES = 56 * 1024 * 1024
ROW_TILES = (1032, 1024, 768, 512, 256, 128, CHUNK)
FF_CHUNKS = ((0, 1024), (1024, 1024), (2048, 768))


def _dot(a, b):
    return jnp.dot(a, b, preferred_element_type=F32)


def _bdot(a, b):
    return _dot(a.astype(BF16), b.astype(BF16))


def _layer_norm(y, g, b):
    mu = jnp.mean(y, axis=-1, keepdims=True)
    d = y - mu
    var = jnp.mean(d * d, axis=-1, keepdims=True)
    return d * lax.rsqrt(var + LN_EPS) * g + b


def _silu(x):
    return x * jax.nn.sigmoid(x)


def _softplus(x):
    return jnp.maximum(x, 0.0) + jnp.log(1.0 + jnp.exp(-jnp.abs(x)))


def _tri_masks():
    row = lax.broadcasted_iota(jnp.int32, (CHUNK, CHUNK), 0)
    col = lax.broadcasted_iota(jnp.int32, (CHUNK, CHUNK), 1)
    return row >= col, row > col, row == col


def _split2(x):
    hi = x.astype(BF16)
    lo = (x - hi.astype(F32)).astype(BF16)
    return hi, lo


def _split3(x):
    hi = x.astype(BF16)
    r1 = x - hi.astype(F32)
    mid = r1.astype(BF16)
    lo = (r1 - mid.astype(F32)).astype(BF16)
    return hi, mid, lo


def _mm3(a, b):
    ah, al = _split2(a)
    bh, bl = _split2(b)
    return _dot(ah, bh) + (_dot(ah, bl) + _dot(al, bh))


def _mm_exact_lhs(a_bf16, pieces):
    out = _dot(a_bf16, pieces[0])
    for p in pieces[1:]:
        out = out + _dot(a_bf16, p)
    return out


def _mm_exact_rhs(pieces, b_bf16):
    out = _dot(pieces[0], b_bf16)
    for p in pieces[1:]:
        out = out + _dot(p, b_bf16)
    return out


def _row_tile(n):
    for cand in ROW_TILES:
        if n % cand == 0:
            return cand
    return n


def _const_spec(shape):
    return pl.BlockSpec(shape, lambda *_: (0,) * len(shape), pipeline_mode=pl.Buffered(1))


def _row_spec(tile, width):
    return pl.BlockSpec((1, tile, width), lambda b, j: (b, j, 0))


def _ffn_ln_kernel(x_ref, win_ref, wout_ref, g_ref, b_ref, o_ref):
    x = x_ref[0]
    xb = x.astype(BF16)
    acc = jnp.zeros(x.shape, F32)
    for start, width in FF_CHUNKS:
        gate = _dot(xb, win_ref[:, start:start + width])
        up = _dot(xb, win_ref[:, D_FF + start:D_FF + start + width])
        act = (_silu(gate) * up).astype(BF16)
        acc = acc + _dot(act, wout_ref[start:start + width, :])
    o_ref[0] = _layer_norm(ALPHA * x + 0.5 * acc, g_ref[...], b_ref[...])


def _ffn_ln(h3, w_in, w_out, g, b, out_seq=None):
    bsz, seq, _ = h3.shape
    out_seq = seq if out_seq is None else out_seq
    tile = _row_tile(min(seq, out_seq))
    return pl.pallas_call(
        _ffn_ln_kernel,
        grid=(bsz, min(seq, out_seq) // tile),
        in_specs=[_row_spec(tile, D_MODEL),
                  _const_spec((D_MODEL, 2 * D_FF)),
                  _const_spec((D_FF, D_MODEL)),
                  _const_spec((1, D_MODEL)),
                  _const_spec((1, D_MODEL))],
        out_specs=_row_spec(tile, D_MODEL),
        out_shape=jax.ShapeDtypeStruct((bsz, out_seq, D_MODEL), F32),
        compiler_params=pltpu.CompilerParams(dimension_semantics=("parallel", "parallel"),
                                             vmem_limit_bytes=VMEM_LIMIT_BYTES),
        name="ffn_ln",
    )(h3, w_in, w_out, g, b)


def _in_proj_kernel(h_ref, wa_ref, wb_ref, wc_ref, pa_ref, pb_ref, pc_ref, *, tile, seq):
    pos = pl.program_id(1) * tile + lax.broadcasted_iota(jnp.int32, (tile, 1), 0)
    inert = (pos >= seq - CHUNK) & (pos < seq - N_META)
    hb = jnp.where(inert, 0.0, h_ref[0]).astype(BF16)
    pa_ref[0] = _dot(hb, wa_ref[...])
    pb_ref[0] = _dot(hb, wb_ref[...])
    pc_ref[0] = _dot(hb, wc_ref[...])


def _in_proj(h, wa, wb, wc):
    bsz, seq, _ = h.shape
    tile = _row_tile(seq)
    return pl.pallas_call(
        functools.partial(_in_proj_kernel, tile=tile, seq=seq),
        grid=(bsz, seq // tile),
        in_specs=[_row_spec(tile, D_MODEL), _const_spec((D_MODEL, D_A_COLS)),
                  _const_spec((D_MODEL, D_B_IN)), _const_spec((D_MODEL, D_C_IN))],
        out_specs=[_row_spec(tile, D_A_COLS), _row_spec(tile, D_B_IN), _row_spec(tile, D_C_IN)],
        out_shape=[jax.ShapeDtypeStruct((bsz, seq, D_A_COLS), F32),
                   jax.ShapeDtypeStruct((bsz, seq, D_B_IN), F32),
                   jax.ShapeDtypeStruct((bsz, seq, D_C_IN), F32)],
        compiler_params=pltpu.CompilerParams(dimension_semantics=("parallel", "parallel"),
                                             vmem_limit_bytes=VMEM_LIMIT_BYTES),
        name="in_proj",
    )(h, wa, wb, wc)


def _out_proj_ln_kernel(h_ref, ya_ref, yb_ref, yc_ref, w_ref, g_ref, b_ref, o_ref):
    mix = _dot(ya_ref[0], w_ref[0:D_A, :])
    mix = mix + _dot(yb_ref[0], w_ref[D_A:D_A + D_B, :])
    mix = mix + _dot(yc_ref[0], w_ref[D_A + D_B:, :])
    o_ref[0] = _layer_norm(ALPHA * h_ref[0] + mix, g_ref[...], b_ref[...])


def _out_proj_ln(h, ya, yb, yc, w_out, g, b):
    bsz, seq, _ = h.shape
    tile = _row_tile(seq)
    return pl.pallas_call(
        _out_proj_ln_kernel,
        grid=(bsz, seq // tile),
        in_specs=[_row_spec(tile, D_MODEL), _row_spec(tile, D_A), _row_spec(tile, D_B), _row_spec(tile, D_C),
                  _const_spec((D_A + D_B + D_C, D_MODEL)),
                  _const_spec((1, D_MODEL)), _const_spec((1, D_MODEL))],
        out_specs=_row_spec(tile, D_MODEL),
        out_shape=jax.ShapeDtypeStruct((bsz, seq, D_MODEL), F32),
        compiler_params=pltpu.CompilerParams(dimension_semantics=("parallel", "parallel"),
                                             vmem_limit_bytes=VMEM_LIMIT_BYTES),
        name="out_proj_ln",
    )(h, ya, yb, yc, w_out, g, b)


def _bmm(a, b):
    return jnp.einsum('nij,njk->nik', a, b, preferred_element_type=F32)


def _bmm_nt(a, b):
    return jnp.einsum('nik,njk->nij', a, b, preferred_element_type=F32)


def _bmm_tn(a, b):
    return jnp.einsum('nki,nkj->nij', a, b, preferred_element_type=F32)


def _b1(mm, a, b):
    return mm(a.astype(BF16), b.astype(BF16))


def _chain_series(x, eye):
    p = eye + x
    for _ in range(5):
        x = _b1(_bmm, x, x)
        p = p + _b1(_bmm, p, x)
    return p


def _pair_diag(y):
    left = lax.broadcasted_iota(jnp.int32, (1, 1, 2 * RWKV_HEAD), 2) < RWKV_HEAD
    return jnp.concatenate([jnp.where(left, y, 0.0), jnp.where(left, 0.0, y)], axis=1)


def _pair_series(x, eye):
    p = eye + x
    xd = _pair_diag(x).astype(BF16)
    for _ in range(5):
        x = _bmm(x.astype(BF16), xd)
        xd = _pair_diag(x).astype(BF16)
        p = p + _bmm(p.astype(BF16), xd)
    return p


def _to_chains(x3, heads, width, offset=0):
    return jnp.concatenate([x3[:, :, offset + h * width:offset + (h + 1) * width] for h in range(heads)], axis=0)


def _from_chains(xn, heads):
    bsz = xn.shape[0] // heads
    return jnp.concatenate([xn[h * bsz:(h + 1) * bsz] for h in range(heads)], axis=2)


def _block_tril(rows):
    r = lax.broadcasted_iota(jnp.int32, (rows, rows), 0)
    c = lax.broadcasted_iota(jnp.int32, (rows, rows), 1)
    return ((r // CHUNK == c // CHUNK) & (r >= c)).astype(BF16)


def _head_group_matrix(width, head, value):
    r = lax.broadcasted_iota(jnp.int32, (width, width), 0) // head
    c = lax.broadcasted_iota(jnp.int32, (width, width), 1) // head
    return jnp.where(r == c, value, 0.0).astype(BF16)


def _gdn_body(pa_ref, conv_ref, alog_ref, dt_ref, nw_ref, o_ref, cbuf, state):
    bsz = pa_ref.shape[0]
    rows = bsz * CHUNK
    causal, strict, diag = _tri_masks()
    eye = diag.astype(F32)

    x = pa_ref[:, :, 0:GDN_QKV]
    cbuf[:, 8:8 + CHUNK, :] = x
    y = conv_ref[3:4, :] * x
    for tap in range(GDN_CONV - 1):
        lo = 8 - (GDN_CONV - 1) + tap
        y = y + conv_ref[tap:tap + 1, :] * cbuf[:, lo:lo + CHUNK, :]
    cbuf[:, 0:8, :] = x[:, CHUNK - 8:, :]
    qkv = _silu(y)

    yield
    gates = pa_ref[:, :, GDN_QKV + D_A:].reshape(rows, GATE_PAD)
    beta_all = jax.nn.sigmoid(gates).reshape(bsz, CHUNK, GATE_PAD)
    lg_all = -jnp.exp(alog_ref[...]) * _softplus(gates + dt_ref[...])
    g_all = _mm_exact_lhs(_block_tril(rows), _split3(lg_all))
    g_t = g_all.T
    g_all = g_all.reshape(bsz, CHUNK, GATE_PAD)
    beta = jnp.concatenate([beta_all[:, :, h:h + 1] for h in range(GDN_HEADS)], axis=0)
    gc = jnp.concatenate([g_all[:, :, GDN_HEADS + h:GDN_HEADS + h + 1] for h in range(GDN_HEADS)], axis=0)
    g_row = jnp.concatenate([g_t[GDN_HEADS + h:GDN_HEADS + h + 1, b * CHUNK:(b + 1) * CHUNK][None]
                             for h in range(GDN_HEADS) for b in range(bsz)], axis=0)

    yield
    q = _to_chains(qkv, GDN_HEADS, GDN_DK)
    k = _to_chains(qkv, GDN_HEADS, GDN_DK, GDN_HEADS * GDN_DK)
    v = _to_chains(qkv, GDN_HEADS, GDN_DV, 2 * GDN_HEADS * GDN_DK)
    q = q * (lax.rsqrt(jnp.sum(q * q, -1, keepdims=True) + 1e-6) * GDN_DK ** -0.5)
    k = k * lax.rsqrt(jnp.sum(k * k, -1, keepdims=True) + 1e-6)
    diff = gc - g_row
    decay = jnp.where(causal, jnp.exp(jnp.where(causal, diff, 0.0)), 0.0)
    yield
    kb = k * beta
    kq = _b1(_bmm_nt, jnp.concatenate([kb, q], axis=1), k)
    a_mat = jnp.where(strict, kq[:, :CHUNK] * decay, 0.0)
    qk = kq[:, CHUNK:] * decay
    t_inv = _chain_series(-a_mat, eye)
    yield
    e_g = jnp.exp(gc)
    sol = _b1(_bmm, t_inv, jnp.concatenate([v * beta, kb * e_g], axis=2))
    u, w = sol[:, :, :GDN_DV], sol[:, :, GDN_DV:]
    g_last = gc[:, CHUNK - 1:CHUNK, :]
    k_dec = k * jnp.exp(g_last - gc)
    s0 = state[...]
    ws = _b1(_bmm, jnp.concatenate([w, q * e_g], axis=1), s0)
    v_new = u - ws[:, :CHUNK]
    o = ws[:, CHUNK:] + _b1(_bmm, qk, v_new)
    state[...] = s0 * jnp.exp(g_last) + _b1(_bmm_tn, k_dec, v_new)
    yield
    o = o * lax.rsqrt(jnp.mean(o * o, -1, keepdims=True) + LN_EPS) * nw_ref[...]
    z = _to_chains(pa_ref[:, :, GDN_QKV:GDN_QKV + D_A], GDN_HEADS, GDN_DV)
    o_ref[...] = _from_chains(o * _silu(z), GDN_HEADS).astype(o_ref.dtype)


def _rwkv_body(pb_ref, mu_ref, w0_ref, wup_ref, a0_ref, aup_ref, gup_ref, kk_ref, ka_ref,
               rk_ref, lg_ref, lb_ref, o_ref, sbuf, state):
    bsz = pb_ref.shape[0]
    rows = bsz * CHUNK
    head_ones = _head_group_matrix(D_B, RWKV_HEAD, 1.0)
    head_avg = _head_group_matrix(D_B, RWKV_HEAD, 1.0 / RWKV_HEAD)

    x = pb_ref[...]
    sbuf[:, 8:8 + CHUNK, :] = x
    prev = sbuf[:, 7:7 + CHUNK, :]
    sbuf[:, 0:8, :] = x[:, CHUNK - 8:, :]
    pf = (x + (prev - x) * mu_ref[...]).reshape(rows, D_B_IN)
    r_all, k_all, v_all = pf[:, 0:D_B], pf[:, D_B:2 * D_B], pf[:, 2 * D_B:3 * D_B]
    c0 = 3 * D_B
    xw = pf[:, c0:c0 + RWKV_DECAY_RANK]
    xa = pf[:, c0 + RWKV_DECAY_RANK:c0 + RWKV_DECAY_RANK + RWKV_A_RANK]
    xg = pf[:, c0 + RWKV_DECAY_RANK + RWKV_A_RANK:]
    log_w = -_softplus(-(w0_ref[...] + _mm3(jnp.tanh(xw), wup_ref[...]))) - 0.5
    wlog = -jnp.exp(log_w)
    a_all = jax.nn.sigmoid(a0_ref[...] + _mm3(xa, aup_ref[...]))
    gate_all = _bdot(jax.nn.sigmoid(xg), gup_ref[...])
    yield
    kk = k_all * kk_ref[...]
    kk = kk * lax.rsqrt(_mm_exact_rhs(_split2(kk * kk), head_ones) + 1e-6)
    k2 = k_all * (1.0 + (a_all - 1.0) * ka_ref[...])
    bb = kk * a_all
    gc = _mm_exact_lhs(_block_tril(rows), _split3(wlog))
    e_neg = jnp.exp(-gc)
    r_t = r_all * jnp.exp(gc)
    a_t = -kk * jnp.exp(gc - wlog)
    k_t = k2 * e_neg
    b_t = bb * e_neg
    gc3 = gc.reshape(bsz, CHUNK, D_B)
    g_last = gc3[:, CHUNK - 1:CHUNK, :]
    e_tail = jnp.exp(g_last - gc3).reshape(rows, D_B)
    bonus = _mm_exact_rhs(_split2(r_all * k2 * rk_ref[...]), head_ones) * v_all

    yield
    pair_w = 2 * RWKV_HEAD
    n_pairs = RWKV_HEADS // 2
    pairs = lambda t: _to_chains(t.reshape(bsz, CHUNK, D_B), n_pairs, pair_w)
    a_n, r_n, b_n, k_n, v_n = pairs(a_t), pairs(r_t), pairs(b_t), pairs(k_t), pairs(v_all)
    bdec_n, kdec_n = pairs(bb * e_tail), pairs(k2 * e_tail)
    elast_n = _to_chains(jnp.exp(g_last), n_pairs, pair_w)
    prow = lax.broadcasted_iota(jnp.int32, (CHUNK, pair_w), 0)
    pcol = lax.broadcasted_iota(jnp.int32, (CHUNK, pair_w), 1) % RWKV_HEAD
    causal2, strict2, eye2 = prow >= pcol, prow > pcol, (prow == pcol).astype(F32)
    srow = lax.broadcasted_iota(jnp.int32, (pair_w, pair_w), 0) // RWKV_HEAD
    scol = lax.broadcasted_iota(jnp.int32, (pair_w, pair_w), 1) // RWKV_HEAD
    same_head = srow == scol

    ar = jnp.concatenate([a_n, r_n], axis=1).astype(BF16)
    xb = _bmm_nt(ar, _pair_diag(b_n).astype(BF16))
    xk = _bmm_nt(ar, _pair_diag(k_n).astype(BF16))
    a_ab = jnp.where(strict2, xb[:, :CHUNK], 0.0)
    a_rb = jnp.where(causal2, xb[:, CHUNK:], 0.0)
    a_k = jnp.concatenate([jnp.where(strict2, xk[:, :CHUNK], 0.0), jnp.where(causal2, xk[:, CHUNK:], 0.0)], axis=1)
    yield
    m_inv = _pair_series(a_ab, eye2).astype(BF16)
    yield
    akv = _bmm(a_k.astype(BF16), _pair_diag(v_n).astype(BF16))
    w_mat = _bmm(m_inv, _pair_diag(a_n).astype(BF16))
    u2 = _bmm(m_inv, _pair_diag(akv[:, :CHUNK]).astype(BF16))
    s0 = state[...]
    wr = _b1(_bmm_nt, jnp.concatenate([w_mat, r_n], axis=1), s0)
    u = wr[:, :CHUNK] + u2
    y_n = wr[:, CHUNK:] + _b1(_bmm, a_rb, _pair_diag(u)) + akv[:, CHUNK:]
    upd = _b1(_bmm_tn, jnp.concatenate([u, v_n], axis=1), jnp.concatenate([bdec_n, kdec_n], axis=1))
    state[...] = s0 * elast_n + jnp.where(same_head, upd, 0.0)

    yield
    y = _from_chains(y_n, n_pairs).reshape(rows, D_B)
    mu = _mm_exact_rhs(_split2(y), head_avg)
    d = y - mu
    var = _mm_exact_rhs(_split2(d * d), head_avg)
    yn = d * lax.rsqrt(var + RWKV_LNX_EPS) * lg_ref[...] + lb_ref[...]
    o_ref[...] = ((yn + bonus) * gate_all).reshape(bsz, CHUNK, D_B).astype(o_ref.dtype)


def _ret_log_gamma(h):
    return math.log(1.0 - 2.0 ** (-5.0 - h))


def _ret_body(pc_ref, cos_ref, sin_ref, ng_ref, nb_ref, o_ref, state):
    bsz = pc_ref.shape[0]
    rows = bsz * CHUNK
    qk_w = RET_HEADS * RET_DK
    lane = lax.broadcasted_iota(jnp.int32, (bsz, CHUNK, qk_w), 2)
    first_half = (lane % RET_DK) < (RET_DK // 2)

    def rope(t):
        partner = jnp.where(first_half, pltpu.roll(t, qk_w - RET_DK // 2, 2), pltpu.roll(t, RET_DK // 2, 2))
        return t * cos_ref[...] + partner * sin_ref[...]

    row = lax.broadcasted_iota(jnp.int32, (CHUNK, CHUNK), 0)
    col = lax.broadcasted_iota(jnp.int32, (CHUNK, CHUNK), 1)
    delta = (row - col).astype(F32)
    idx = lax.broadcasted_iota(jnp.int32, (CHUNK, 1), 0).astype(F32)
    head_avg = _head_group_matrix(D_C, RET_DV, 1.0 / RET_DV)
    rep = lambda t: jnp.broadcast_to(t[None], (bsz,) + t.shape)
    d_intra = jnp.concatenate(
        [rep(jnp.where(delta >= 0, jnp.exp(_ret_log_gamma(h) * jnp.maximum(delta, 0.0)), 0.0))
         for h in range(RET_HEADS)], axis=0)
    q_scale = jnp.concatenate([rep(jnp.exp(_ret_log_gamma(h) * (idx + 1.0))) for h in range(RET_HEADS)], axis=0)
    k_scale = jnp.concatenate([rep(jnp.exp(_ret_log_gamma(h) * (CHUNK - 1.0 - idx))) for h in range(RET_HEADS)],
                              axis=0)
    s_scale = jnp.concatenate([jnp.full((bsz, 1, 1), math.exp(_ret_log_gamma(h) * CHUNK), F32)
                               for h in range(RET_HEADS)], axis=0)

    yield
    q = _to_chains(rope(pc_ref[:, :, 0:qk_w]), RET_HEADS, RET_DK)
    k = _to_chains(rope(pc_ref[:, :, qk_w:2 * qk_w]) * RET_DK ** -0.5, RET_HEADS, RET_DK)
    v = _to_chains(pc_ref[:, :, 2 * qk_w:2 * qk_w + D_C], RET_HEADS, RET_DV)
    intra = _b1(_bmm, _b1(_bmm_nt, q, k) * d_intra, v)
    s0 = state[...]
    y_n = intra + _b1(_bmm, q * q_scale, s0)
    state[...] = s0 * s_scale + _b1(_bmm_tn, k * k_scale, v)
    yield
    y = _from_chains(y_n, RET_HEADS).reshape(rows, D_C)
    mu = _mm_exact_rhs(_split2(y), head_avg)
    d = y - mu
    var = _mm_exact_rhs(_split2(d * d), head_avg)
    yn = d * lax.rsqrt(var + LN_EPS) * ng_ref[...] + nb_ref[...]
    gate = pc_ref[:, :, 2 * qk_w + D_C:].reshape(rows, D_C)
    o_ref[...] = (yn * _silu(gate)).reshape(bsz, CHUNK, D_C).astype(o_ref.dtype)


N_GDN_PARAMS, N_RWKV_PARAMS, N_RET_PARAMS = 4, 11, 4


def _mixers_kernel(*refs):
    pa_ref, pb_ref, pc_ref = refs[:3]
    params = refs[3:3 + N_GDN_PARAMS + N_RWKV_PARAMS + N_RET_PARAMS]
    gdn_p, rwkv_p = params[:N_GDN_PARAMS], params[N_GDN_PARAMS:N_GDN_PARAMS + N_RWKV_PARAMS]
    ret_p = params[N_GDN_PARAMS + N_RWKV_PARAMS:]
    oa_ref, ob_ref, oc_ref, cbuf, gdn_state, sbuf, rwkv_state, ret_state = refs[3 + len(params):]

    @pl.when(pl.program_id(0) == 0)
    def _():
        cbuf[:, 0:8, :] = jnp.zeros((cbuf.shape[0], 8, GDN_QKV), F32)
        sbuf[:, 0:8, :] = jnp.zeros((sbuf.shape[0], 8, D_B_IN), F32)
        gdn_state[...] = jnp.zeros(gdn_state.shape, F32)
        rwkv_state[...] = jnp.zeros(rwkv_state.shape, F32)
        ret_state[...] = jnp.zeros(ret_state.shape, F32)

    phases = [_gdn_body(pa_ref, *gdn_p, oa_ref, cbuf, gdn_state),
              _rwkv_body(pb_ref, *rwkv_p, ob_ref, sbuf, rwkv_state),
              _ret_body(pc_ref, *ret_p, oc_ref, ret_state)]
    while phases:
        for body in list(phases):
            try:
                next(body)
            except StopIteration:
                phases.remove(body)


def _mixers(pa, pb, pc, gdn_params, rwkv_params, ret_params):
    bsz, seq, _ = pa.shape
    n_chunks = seq // CHUNK
    chunked = lambda width: pl.BlockSpec((bsz, CHUNK, width), lambda c: (0, (c + n_chunks - 1) % n_chunks, 0))
    rope_spec = pl.BlockSpec((CHUNK, RET_HEADS * RET_DK), lambda c: (c, 0))
    params = tuple(gdn_params) + tuple(rwkv_params) + tuple(ret_params)
    param_specs = [_const_spec(p.shape) for p in gdn_params + rwkv_params] + \
                  [rope_spec, rope_spec] + [_const_spec(p.shape) for p in ret_params[2:]]
    return pl.pallas_call(
        _mixers_kernel,
        grid=(n_chunks,),
        in_specs=[chunked(D_A_COLS), chunked(D_B_IN), chunked(D_C_IN)] + param_specs,
        out_specs=[chunked(D_A), chunked(D_B), chunked(D_C)],
        out_shape=[jax.ShapeDtypeStruct((bsz, seq, D_A), BF16),
                   jax.ShapeDtypeStruct((bsz, seq, D_B), BF16),
                   jax.ShapeDtypeStruct((bsz, seq, D_C), BF16)],
        scratch_shapes=[pltpu.VMEM((bsz, 8 + CHUNK, GDN_QKV), F32),
                        pltpu.VMEM((GDN_HEADS * bsz, GDN_DK, GDN_DV), F32),
                        pltpu.VMEM((bsz, 8 + CHUNK, D_B_IN), F32),
                        pltpu.VMEM((RWKV_HEADS // 2 * bsz, 2 * RWKV_HEAD, 2 * RWKV_HEAD), F32),
                        pltpu.VMEM((RET_HEADS * bsz, RET_DK, RET_DV), F32)],
        compiler_params=pltpu.CompilerParams(dimension_semantics=("arbitrary",),
                                             vmem_limit_bytes=VMEM_LIMIT_BYTES),
        name="mixers",
    )(pa, pb, pc, *params)


def _rope_tables(seq):
    half = RET_DK // 2
    inv_freq = 1.0 / (ROPE_BASE ** jnp.linspace(0.0, 1.0, half, dtype=F32))
    ang = jnp.arange(seq, dtype=F32)[:, None] * inv_freq
    cos, sin = jnp.cos(ang), jnp.sin(ang)
    cos_t = jnp.tile(jnp.concatenate([cos, cos], -1), (1, RET_HEADS))
    sin_t = jnp.tile(jnp.concatenate([-sin, sin], -1), (1, RET_HEADS))
    return cos_t, sin_t


def _row(v):
    return v.reshape(1, -1)


def _pad_gate_vec(v):
    return jnp.zeros((1, GATE_PAD), F32).at[0, GDN_HEADS:2 * GDN_HEADS].set(v)


def _mixer_layer(h3, w_in, w_out, ln_g, ln_b, gdn, rwkv, ret, rope):
    wa = jnp.concatenate([w_in[:, :D_A_IN], jnp.zeros((D_MODEL, D_A_COLS - D_A_IN), w_in.dtype)], axis=1)
    wb = w_in[:, D_A_IN:D_A_IN + D_B_IN]
    wc = w_in[:, D_A_IN + D_B_IN:]
    pa, pb, pc = _in_proj(h3, wa.astype(BF16), wb.astype(BF16), wc.astype(BF16))
    conv_w, a_log, dt_bias, norm_w = gdn
    ya, yb, yc = _mixers(
        pa, pb, pc,
        [conv_w, _pad_gate_vec(a_log), _pad_gate_vec(dt_bias), _row(norm_w)],
        [_row(p) if p.ndim == 1 else p for p in rwkv],
        [rope[0], rope[1], _row(ret[0]), _row(ret[1])])
    return _out_proj_ln(h3, ya, yb, yc, w_out.astype(BF16), _row(ln_g), _row(ln_b))


def kernel(x, meta_tokens, ln_g, ln_b, w_ff1_in, w_ff1_out, w_ff2_in, w_ff2_out, w_in, w_out,
           gdn_conv_w, gdn_a_log, gdn_dt_bias, gdn_norm_w, rwkv_mu, rwkv_w0, rwkv_w_up, rwkv_a0,
           rwkv_a_up, rwkv_g_up, rwkv_k_k, rwkv_k_a, rwkv_r_k, rwkv_lnx_g, rwkv_lnx_b,
           ret_norm_g, ret_norm_b):
    bsz, n_real, d = x.shape
    seq = n_real + CHUNK
    depth = ln_g.shape[0]
    rope = _rope_tables(seq)
    meta_chunk = jnp.concatenate([jnp.zeros((META_PAD, d), x.dtype), meta_tokens.astype(x.dtype)], axis=0)[None]
    h = None
    for l in range(depth):
        ffn1 = (w_ff1_in[l].astype(BF16), w_ff1_out[l].astype(BF16), _row(ln_g[l, 0]), _row(ln_b[l, 0]))
        if l == 0:
            h = _ffn_ln(x, *ffn1, out_seq=seq)
            meta_out = _ffn_ln(meta_chunk, *ffn1)
            h = lax.dynamic_update_slice(h, jnp.broadcast_to(meta_out, (bsz, CHUNK, d)), (0, n_real, 0))
        else:
            h = _ffn_ln(h, *ffn1)
        h = _mixer_layer(
            h, w_in[l], w_out[l], ln_g[l, 1], ln_b[l, 1],
            (gdn_conv_w[l], gdn_a_log[l], gdn_dt_bias[l], gdn_norm_w[l]),
            (rwkv_mu[l], rwkv_w0[l], rwkv_w_up[l], rwkv_a0[l], rwkv_a_up[l], rwkv_g_up[l],
             rwkv_k_k[l], rwkv_k_a[l], rwkv_r_k[l].reshape(-1), rwkv_lnx_g[l], rwkv_lnx_b[l]),
            (ret_norm_g[l], ret_norm_b[l]), rope)
        h = _ffn_ln(h, w_ff2_in[l].astype(BF16), w_ff2_out[l].astype(BF16), _row(ln_g[l, 2]), _row(ln_b[l, 2]),
                    out_seq=n_real if l == depth - 1 else None)
    return h
```

```python
import functools
import math

import jax
import jax.numpy as jnp
from jax import lax
from jax.experimental import pallas as pl
from jax.experimental.pallas import tpu as pltpu

F32 = jnp.float32
BF16 = jnp.bfloat16

D_MODEL = 1024
DEPTH = 2
CHUNK = 64
N_META = 16
META_PAD = CHUNK - N_META
D_FF = 2816
LN_EPS = 1e-5
ALPHA = (2.0 * DEPTH) ** 0.25

GDN_HEADS, GDN_DK, GDN_DV, GDN_CONV = 4, 128, 128, 4
GDN_QKV = 2 * GDN_HEADS * GDN_DK + GDN_HEADS * GDN_DV
D_A = GDN_HEADS * GDN_DV
GATE_PAD = 128
D_A_COLS = GDN_QKV + D_A + GATE_PAD
RWKV_HEADS, RWKV_HEAD = 4, 64
RWKV_DECAY_RANK, RWKV_A_RANK, RWKV_GATE_RANK = 32, 32, 64
RWKV_LNX_EPS = 64e-5
D_B = RWKV_HEADS * RWKV_HEAD
D_B_IN = 3 * D_B + RWKV_DECAY_RANK + RWKV_A_RANK + RWKV_GATE_RANK
RET_HEADS, RET_DK, RET_DV = 4, 32, 64
ROPE_BASE = 10000.0
D_C = RET_HEADS * RET_DV
D_C_IN = 2 * RET_HEADS * RET_DK + 2 * D_C
D_A_IN = GDN_QKV + D_A + 2 * GDN_HEADS

VMEM_LIMIT_BYTES = 56 * 1024 * 1024
ROW_TILES = (1032, 1024, 768, 512, 256, 128, CHUNK)
FF_CHUNKS = ((0, 1024), (1024, 1024), (2048, 768))


def _dot(a, b):
    return jnp.dot(a, b, preferred_element_type=F32)


def _bdot(a, b):
    return _dot(a.astype(BF16), b.astype(BF16))


def _layer_norm(y, g, b):
    mu = jnp.mean(y, axis=-1, keepdims=True)
    d = y - mu
    var = jnp.mean(d * d, axis=-1, keepdims=True)
    return d * lax.rsqrt(var + LN_EPS) * g + b


def _silu(x):
    return x * jax.nn.sigmoid(x)


def _softplus(x):
    return jnp.maximum(x, 0.0) + jnp.log(1.0 + jnp.exp(-jnp.abs(x)))


def _tri_masks():
    row = lax.broadcasted_iota(jnp.int32, (CHUNK, CHUNK), 0)
    col = lax.broadcasted_iota(jnp.int32, (CHUNK, CHUNK), 1)
    return row >= col, row > col, row == col


def _split2(x):
    hi = x.astype(BF16)
    lo = (x - hi.astype(F32)).astype(BF16)
    return hi, lo


def _split3(x):
    hi = x.astype(BF16)
    r1 = x - hi.astype(F32)
    mid = r1.astype(BF16)
    lo = (r1 - mid.astype(F32)).astype(BF16)
    return hi, mid, lo


def _mm3(a, b):
    ah, al = _split2(a)
    bh, bl = _split2(b)
    return _dot(ah, bh) + (_dot(ah, bl) + _dot(al, bh))


def _mm_exact_lhs(a_bf16, pieces):
    out = _dot(a_bf16, pieces[0])
    for p in pieces[1:]:
        out = out + _dot(a_bf16, p)
    return out


def _row_tile(n):
    for cand in ROW_TILES:
        if n % cand == 0:
            return cand
    return n


def _const_spec(shape):
    return pl.BlockSpec(shape, lambda *_: (0,) * len(shape), pipeline_mode=pl.Buffered(1))


def _row_spec(tile, width):
    return pl.BlockSpec((1, tile, width), lambda b, j: (b, j, 0))


def _swiglu_ln(x, win_ref, wout_ref, g_ref, b_ref):
    xb = x.astype(BF16)
    acc = jnp.zeros(x.shape, F32)
    for start, width in FF_CHUNKS:
        gate = _dot(xb, win_ref[:, start:start + width])
        up = _dot(xb, win_ref[:, D_FF + start:D_FF + start + width])
        act = (_silu(gate) * up).astype(BF16)
        acc = acc + _dot(act, wout_ref[start:start + width, :])
    return _layer_norm(ALPHA * x + 0.5 * acc, g_ref[...], b_ref[...])


def _ffn_ln_kernel(x_ref, win_ref, wout_ref, g_ref, b_ref, o_ref):
    o_ref[0] = _swiglu_ln(x_ref[0], win_ref, wout_ref, g_ref, b_ref)


def _ffn_ln(h3, w_in, w_out, g, b, out_seq=None):
    bsz, seq, _ = h3.shape
    out_seq = seq if out_seq is None else out_seq
    tile = _row_tile(min(seq, out_seq))
    return pl.pallas_call(
        _ffn_ln_kernel,
        grid=(bsz, min(seq, out_seq) // tile),
        in_specs=[_row_spec(tile, D_MODEL),
                  _const_spec((D_MODEL, 2 * D_FF)),
                  _const_spec((D_FF, D_MODEL)),
                  _const_spec((1, D_MODEL)),
                  _const_spec((1, D_MODEL))],
        out_specs=_row_spec(tile, D_MODEL),
        out_shape=jax.ShapeDtypeStruct((bsz, out_seq, D_MODEL), F32),
        compiler_params=pltpu.CompilerParams(dimension_semantics=("parallel", "parallel"),
                                             vmem_limit_bytes=VMEM_LIMIT_BYTES),
        name="ffn_ln",
    )(h3, w_in, w_out, g, b)


def _in_proj_kernel(h_ref, wa_ref, wb_ref, wc_ref, pa_ref, pb_ref, pc_ref, *, tile, seq):
    pos = pl.program_id(1) * tile + lax.broadcasted_iota(jnp.int32, (tile, 1), 0)
    inert = (pos >= seq - CHUNK) & (pos < seq - N_META)
    hb = jnp.where(inert, 0.0, h_ref[0]).astype(BF16)
    pa_ref[0] = _dot(hb, wa_ref[...])
    pb_ref[0] = _dot(hb, wb_ref[...])
    pc_ref[0] = _dot(hb, wc_ref[...])


def _in_proj(h, wa, wb, wc):
    bsz, seq, _ = h.shape
    tile = _row_tile(seq)
    return pl.pallas_call(
        functools.partial(_in_proj_kernel, tile=tile, seq=seq),
        grid=(bsz, seq // tile),
        in_specs=[_row_spec(tile, D_MODEL), _const_spec((D_MODEL, D_A_COLS)),
                  _const_spec((D_MODEL, D_B_IN)), _const_spec((D_MODEL, D_C_IN))],
        out_specs=[_row_spec(tile, D_A_COLS), _row_spec(tile, D_B_IN), _row_spec(tile, D_C_IN)],
        out_shape=[jax.ShapeDtypeStruct((bsz, seq, D_A_COLS), F32),
                   jax.ShapeDtypeStruct((bsz, seq, D_B_IN), F32),
                   jax.ShapeDtypeStruct((bsz, seq, D_C_IN), F32)],
        compiler_params=pltpu.CompilerParams(dimension_semantics=("parallel", "parallel"),
                                             vmem_limit_bytes=VMEM_LIMIT_BYTES),
        name="in_proj",
    )(h, wa, wb, wc)


def _mix_ffn_ln_kernel(h_ref, ya_ref, yb_ref, yc_ref, wmix_ref, g1_ref, b1_ref, win_ref, wout_ref,
                       g2_ref, b2_ref, o_ref):
    mix = _dot(ya_ref[0], wmix_ref[0:D_A, :])
    mix = mix + _dot(yb_ref[0], wmix_ref[D_A:D_A + D_B, :])
    mix = mix + _dot(yc_ref[0], wmix_ref[D_A + D_B:, :])
    h2 = _layer_norm(ALPHA * h_ref[0] + mix, g1_ref[...], b1_ref[...])
    o_ref[0] = _swiglu_ln(h2, win_ref, wout_ref, g2_ref, b2_ref)


def _mix_ffn_ln(h, ya, yb, yc, w_mix, g1, b1, w_in, w_out, g2, b2, out_seq=None):
    bsz, seq, _ = h.shape
    out_seq = seq if out_seq is None else out_seq
    tile = _row_tile(min(seq, out_seq))
    vec = _const_spec((1, D_MODEL))
    return pl.pallas_call(
        _mix_ffn_ln_kernel,
        grid=(bsz, min(seq, out_seq) // tile),
        in_specs=[_row_spec(tile, D_MODEL), _row_spec(tile, D_A), _row_spec(tile, D_B), _row_spec(tile, D_C),
                  _const_spec((D_A + D_B + D_C, D_MODEL)), vec, vec,
                  _const_spec((D_MODEL, 2 * D_FF)), _const_spec((D_FF, D_MODEL)), vec, vec],
        out_specs=_row_spec(tile, D_MODEL),
        out_shape=jax.ShapeDtypeStruct((bsz, out_seq, D_MODEL), F32),
        compiler_params=pltpu.CompilerParams(dimension_semantics=("parallel", "parallel"),
                                             vmem_limit_bytes=VMEM_LIMIT_BYTES),
        name="mix_ffn_ln",
    )(h, ya, yb, yc, w_mix, g1, b1, w_in, w_out, g2, b2)


def _bmm(a, b):
    return jnp.einsum('nij,njk->nik', a, b, preferred_element_type=F32)


def _bmm_nt(a, b):
    return jnp.einsum('nik,njk->nij', a, b, preferred_element_type=F32)


def _bmm_tn(a, b):
    return jnp.einsum('nki,nkj->nij', a, b, preferred_element_type=F32)


def _b1(mm, a, b):
    return mm(a.astype(BF16), b.astype(BF16))


def _pair_diag(y):
    left = lax.broadcasted_iota(jnp.int32, (1, 1, 2 * RWKV_HEAD), 2) < RWKV_HEAD
    return jnp.concatenate([jnp.where(left, y, 0.0), jnp.where(left, 0.0, y)], axis=1)


def _pair_series(x, eye):
    p = eye + x
    xd = _pair_diag(x).astype(BF16)
    for _ in range(5):
        x = _bmm(x.astype(BF16), xd)
        xd = _pair_diag(x).astype(BF16)
        p = p + _bmm(p.astype(BF16), xd)
    return p


def _to_chains(x3, heads, width, offset=0):
    return jnp.concatenate([x3[:, :, offset + h * width:offset + (h + 1) * width] for h in range(heads)], axis=0)


def _from_chains(xn, heads):
    bsz = xn.shape[0] // heads
    return jnp.concatenate([xn[h * bsz:(h + 1) * bsz] for h in range(heads)], axis=2)


def _block_tril(rows):
    r = lax.broadcasted_iota(jnp.int32, (rows, rows), 0)
    c = lax.broadcasted_iota(jnp.int32, (rows, rows), 1)
    return ((r // CHUNK == c // CHUNK) & (r >= c)).astype(BF16)


def _head_group_matrix(width, head, value):
    r = lax.broadcasted_iota(jnp.int32, (width, width), 0) // head
    c = lax.broadcasted_iota(jnp.int32, (width, width), 1) // head
    return jnp.where(r == c, value, 0.0).astype(BF16)


def _gdn_body(pa_ref, conv_ref, alog_ref, dt_ref, nw_ref, o_ref, cbuf, state):
    bsz = pa_ref.shape[0]
    rows = bsz * CHUNK
    causal, strict, _ = _tri_masks()

    x = pa_ref[:, :, 0:GDN_QKV]
    cbuf[:, 8:8 + CHUNK, :] = x
    y = conv_ref[3:4, :] * x
    for tap in range(GDN_CONV - 1):
        lo = 8 - (GDN_CONV - 1) + tap
        y = y + conv_ref[tap:tap + 1, :] * cbuf[:, lo:lo + CHUNK, :]
    cbuf[:, 0:8, :] = x[:, CHUNK - 8:, :]
    qkv = _silu(y)

    yield
    gates = pa_ref[:, :, GDN_QKV + D_A:].reshape(rows, GATE_PAD)
    beta_all = jax.nn.sigmoid(gates).reshape(bsz, CHUNK, GATE_PAD)
    lg_all = -jnp.exp(alog_ref[...]) * _softplus(gates + dt_ref[...])
    g_all = _mm_exact_lhs(_block_tril(rows), _split3(lg_all))
    g_t = g_all.T
    g_all = g_all.reshape(bsz, CHUNK, GATE_PAD)
    beta = jnp.concatenate([beta_all[:, :, h:h + 1] for h in range(GDN_HEADS)], axis=0)
    gc = jnp.concatenate([g_all[:, :, GDN_HEADS + h:GDN_HEADS + h + 1] for h in range(GDN_HEADS)], axis=0)
    g_row = jnp.concatenate([g_t[GDN_HEADS + h:GDN_HEADS + h + 1, b * CHUNK:(b + 1) * CHUNK][None]
                             for h in range(GDN_HEADS) for b in range(bsz)], axis=0)

    yield
    q = _to_chains(qkv, GDN_HEADS, GDN_DK)
    k = _to_chains(qkv, GDN_HEADS, GDN_DK, GDN_HEADS * GDN_DK)
    v = _to_chains(qkv, GDN_HEADS, GDN_DV, 2 * GDN_HEADS * GDN_DK)
    q = q * (lax.rsqrt(jnp.sum(q * q, -1, keepdims=True) + 1e-6) * GDN_DK ** -0.5)
    k = k * lax.rsqrt(jnp.sum(k * k, -1, keepdims=True) + 1e-6)
    diff = gc - g_row
    decay = jnp.where(causal, jnp.exp(jnp.where(causal, diff, 0.0)), 0.0)
    yield
    kb = k * beta
    kq = _b1(_bmm_nt, jnp.concatenate([kb, q], axis=1), k)
    a_mat = jnp.where(strict, kq[:, :CHUNK] * decay, 0.0)
    qk = kq[:, CHUNK:] * decay
    half = a_mat.shape[0] // 2
    packed = yield jnp.concatenate([-a_mat[:half], -a_mat[half:]], axis=2)
    t_inv = jnp.concatenate([packed[:, :, :CHUNK], packed[:, :, CHUNK:]], axis=0)
    e_g = jnp.exp(gc)
    sol = _b1(_bmm, t_inv, jnp.concatenate([v * beta, kb * e_g], axis=2))
    u, w = sol[:, :, :GDN_DV], sol[:, :, GDN_DV:]
    g_last = gc[:, CHUNK - 1:CHUNK, :]
    k_dec = k * jnp.exp(g_last - gc)
    s0 = state[...]
    ws = _b1(_bmm, jnp.concatenate([w, q * e_g], axis=1), s0)
    v_new = u - ws[:, :CHUNK]
    o = ws[:, CHUNK:] + _b1(_bmm, qk, v_new)
    state[...] = s0 * jnp.exp(g_last) + _b1(_bmm_tn, k_dec, v_new)
    yield
    o = o * lax.rsqrt(jnp.mean(o * o, -1, keepdims=True) + LN_EPS) * nw_ref[...]
    z = _to_chains(pa_ref[:, :, GDN_QKV:GDN_QKV + D_A], GDN_HEADS, GDN_DV)
    o_ref[...] = _from_chains(o * _silu(z), GDN_HEADS).astype(o_ref.dtype)


def _rwkv_body(pb_ref, mu_ref, w0_ref, wup_ref, a0_ref, aup_ref, gup_ref, kk_ref, ka_ref,
               rk_ref, lg_ref, lb_ref, o_ref, sbuf, state):
    bsz = pb_ref.shape[0]
    rows = bsz * CHUNK
    head_ones = _head_group_matrix(D_B, RWKV_HEAD, 1.0)
    head_avg = _head_group_matrix(D_B, RWKV_HEAD, 1.0 / RWKV_HEAD)

    x = pb_ref[...]
    sbuf[:, 8:8 + CHUNK, :] = x
    prev = sbuf[:, 7:7 + CHUNK, :]
    sbuf[:, 0:8, :] = x[:, CHUNK - 8:, :]
    pf = (x + (prev - x) * mu_ref[...]).reshape(rows, D_B_IN)
    r_all, k_all, v_all = pf[:, 0:D_B], pf[:, D_B:2 * D_B], pf[:, 2 * D_B:3 * D_B]
    c0 = 3 * D_B
    xw = pf[:, c0:c0 + RWKV_DECAY_RANK]
    xa = pf[:, c0 + RWKV_DECAY_RANK:c0 + RWKV_DECAY_RANK + RWKV_A_RANK]
    xg = pf[:, c0 + RWKV_DECAY_RANK + RWKV_A_RANK:]
    log_w = -_softplus(-(w0_ref[...] + _mm3(jnp.tanh(xw), wup_ref[...]))) - 0.5
    wlog = -jnp.exp(log_w)
    a_all = jax.nn.sigmoid(a0_ref[...] + _mm3(xa, aup_ref[...]))
    gate_all = _bdot(jax.nn.sigmoid(xg), gup_ref[...])
    yield
    kk = k_all * kk_ref[...]
    kk = kk * lax.rsqrt(_bdot(kk * kk, head_ones) + 1e-6)
    k2 = k_all * (1.0 + (a_all - 1.0) * ka_ref[...])
    bb = kk * a_all
    gc = _mm_exact_lhs(_block_tril(rows), _split2(wlog))
    e_neg = jnp.exp(-gc)
    r_t = r_all * jnp.exp(gc)
    a_t = -kk * jnp.exp(gc - wlog)
    k_t = k2 * e_neg
    b_t = bb * e_neg
    gc3 = gc.reshape(bsz, CHUNK, D_B)
    g_last = gc3[:, CHUNK - 1:CHUNK, :]
    e_tail = jnp.exp(g_last - gc3).reshape(rows, D_B)
    bonus = _bdot(r_all * k2 * rk_ref[...], head_ones) * v_all

    yield
    pair_w = 2 * RWKV_HEAD
    n_pairs = RWKV_HEADS // 2
    pairs = lambda t: _to_chains(t.reshape(bsz, CHUNK, D_B), n_pairs, pair_w)
    a_n, r_n, b_n, k_n, v_n = pairs(a_t), pairs(r_t), pairs(b_t), pairs(k_t), pairs(v_all)
    bdec_n, kdec_n = pairs(bb * e_tail), pairs(k2 * e_tail)
    elast_n = _to_chains(jnp.exp(g_last), n_pairs, pair_w)
    prow = lax.broadcasted_iota(jnp.int32, (CHUNK, pair_w), 0)
    pcol = lax.broadcasted_iota(jnp.int32, (CHUNK, pair_w), 1) % RWKV_HEAD
    causal2, strict2, eye2 = prow >= pcol, prow > pcol, (prow == pcol).astype(F32)
    srow = lax.broadcasted_iota(jnp.int32, (pair_w, pair_w), 0) // RWKV_HEAD
    scol = lax.broadcasted_iota(jnp.int32, (pair_w, pair_w), 1) // RWKV_HEAD
    same_head = srow == scol

    ar = jnp.concatenate([a_n, r_n], axis=1).astype(BF16)
    xb = _bmm_nt(ar, _pair_diag(b_n).astype(BF16))
    xk = _bmm_nt(ar, _pair_diag(k_n).astype(BF16))
    a_ab = jnp.where(strict2, xb[:, :CHUNK], 0.0)
    a_rb = jnp.where(causal2, xb[:, CHUNK:], 0.0)
    a_k = jnp.concatenate([jnp.where(strict2, xk[:, :CHUNK], 0.0), jnp.where(causal2, xk[:, CHUNK:], 0.0)], axis=1)
    m_inv = (yield a_ab).astype(BF16)
    akv = _bmm(a_k.astype(BF16), _pair_diag(v_n).astype(BF16))
    w_mat = _bmm(m_inv, _pair_diag(a_n).astype(BF16))
    u2 = _bmm(m_inv, _pair_diag(akv[:, :CHUNK]).astype(BF16))
    s0 = state[...]
    wr = _b1(_bmm_nt, jnp.concatenate([w_mat, r_n], axis=1), s0)
    u = wr[:, :CHUNK] + u2
    y_n = wr[:, CHUNK:] + _b1(_bmm, a_rb, _pair_diag(u)) + akv[:, CHUNK:]
    upd = _b1(_bmm_tn, jnp.concatenate([u, v_n], axis=1), jnp.concatenate([bdec_n, kdec_n], axis=1))
    state[...] = s0 * elast_n + jnp.where(same_head, upd, 0.0)

    yield
    y = _from_chains(y_n, n_pairs).reshape(rows, D_B)
    mu = _bdot(y, head_avg)
    d = y - mu
    var = _bdot(d * d, head_avg)
    yn = d * lax.rsqrt(var + RWKV_LNX_EPS) * lg_ref[...] + lb_ref[...]
    o_ref[...] = ((yn + bonus) * gate_all).reshape(bsz, CHUNK, D_B).astype(o_ref.dtype)


def _ret_log_gamma(h):
    return math.log(1.0 - 2.0 ** (-5.0 - h))


def _ret_body(pc_ref, cos_ref, sin_ref, ng_ref, nb_ref, o_ref, state):
    bsz = pc_ref.shape[0]
    rows = bsz * CHUNK
    qk_w = RET_HEADS * RET_DK
    lane = lax.broadcasted_iota(jnp.int32, (bsz, CHUNK, qk_w), 2)
    first_half = (lane % RET_DK) < (RET_DK // 2)

    def rope(t):
        partner = jnp.where(first_half, pltpu.roll(t, qk_w - RET_DK // 2, 2), pltpu.roll(t, RET_DK // 2, 2))
        return t * cos_ref[...] + partner * sin_ref[...]

    row = lax.broadcasted_iota(jnp.int32, (CHUNK, CHUNK), 0)
    col = lax.broadcasted_iota(jnp.int32, (CHUNK, CHUNK), 1)
    delta = (row - col).astype(F32)
    idx = lax.broadcasted_iota(jnp.int32, (CHUNK, 1), 0).astype(F32)
    head_avg = _head_group_matrix(D_C, RET_DV, 1.0 / RET_DV)
    rep = lambda t: jnp.broadcast_to(t[None], (bsz,) + t.shape)
    d_intra = jnp.concatenate(
        [rep(jnp.where(delta >= 0, jnp.exp(_ret_log_gamma(h) * jnp.maximum(delta, 0.0)), 0.0))
         for h in range(RET_HEADS)], axis=0)
    q_scale = jnp.concatenate([rep(jnp.exp(_ret_log_gamma(h) * (idx + 1.0))) for h in range(RET_HEADS)], axis=0)
    k_scale = jnp.concatenate([rep(jnp.exp(_ret_log_gamma(h) * (CHUNK - 1.0 - idx))) for h in range(RET_HEADS)],
                              axis=0)
    s_scale = jnp.concatenate([jnp.full((bsz, 1, 1), math.exp(_ret_log_gamma(h) * CHUNK), F32)
                               for h in range(RET_HEADS)], axis=0)

    yield
    q = _to_chains(rope(pc_ref[:, :, 0:qk_w]), RET_HEADS, RET_DK)
    k = _to_chains(rope(pc_ref[:, :, qk_w:2 * qk_w]) * RET_DK ** -0.5, RET_HEADS, RET_DK)
    v = _to_chains(pc_ref[:, :, 2 * qk_w:2 * qk_w + D_C], RET_HEADS, RET_DV)
    intra = _b1(_bmm, _b1(_bmm_nt, q, k) * d_intra, v)
    s0 = state[...]
    y_n = intra + _b1(_bmm, q * q_scale, s0)
    state[...] = s0 * s_scale + _b1(_bmm_tn, k * k_scale, v)
    yield
    y = _from_chains(y_n, RET_HEADS).reshape(rows, D_C)
    mu = _bdot(y, head_avg)
    d = y - mu
    var = _bdot(d * d, head_avg)
    yn = d * lax.rsqrt(var + LN_EPS) * ng_ref[...] + nb_ref[...]
    gate = pc_ref[:, :, 2 * qk_w + D_C:].reshape(rows, D_C)
    o_ref[...] = (yn * _silu(gate)).reshape(bsz, CHUNK, D_C).astype(o_ref.dtype)


N_GDN_PARAMS, N_RWKV_PARAMS, N_RET_PARAMS = 4, 11, 4


def _mixers_kernel(*refs):
    pa_ref, pb_ref, pc_ref = refs[:3]
    params = refs[3:3 + N_GDN_PARAMS + N_RWKV_PARAMS + N_RET_PARAMS]
    gdn_p, rwkv_p = params[:N_GDN_PARAMS], params[N_GDN_PARAMS:N_GDN_PARAMS + N_RWKV_PARAMS]
    ret_p = params[N_GDN_PARAMS + N_RWKV_PARAMS:]
    oa_ref, ob_ref, oc_ref, cbuf, gdn_state, sbuf, rwkv_state, ret_state = refs[3 + len(params):]

    @pl.when(pl.program_id(0) == 0)
    def _():
        cbuf[:, 0:8, :] = jnp.zeros((cbuf.shape[0], 8, GDN_QKV), F32)
        sbuf[:, 0:8, :] = jnp.zeros((sbuf.shape[0], 8, D_B_IN), F32)
        gdn_state[...] = jnp.zeros(gdn_state.shape, F32)
        rwkv_state[...] = jnp.zeros(rwkv_state.shape, F32)
        ret_state[...] = jnp.zeros(ret_state.shape, F32)

    bodies = [_gdn_body(pa_ref, *gdn_p, oa_ref, cbuf, gdn_state),
              _rwkv_body(pb_ref, *rwkv_p, ob_ref, sbuf, rwkv_state),
              _ret_body(pc_ref, *ret_p, oc_ref, ret_state)]
    prow = lax.broadcasted_iota(jnp.int32, (CHUNK, 2 * CHUNK), 0)
    pcol = lax.broadcasted_iota(jnp.int32, (CHUNK, 2 * CHUNK), 1) % CHUNK
    eye2 = (prow == pcol).astype(F32)
    inbox = {body: None for body in bodies}
    waiting = {}
    while bodies:
        for body in list(bodies):
            if body in waiting:
                continue
            try:
                request = body.send(inbox[body])
            except StopIteration:
                bodies.remove(body)
                continue
            inbox[body] = None
            if request is not None:
                waiting[body] = request
        if len(waiting) == 2 or (waiting and all(body in waiting for body in bodies)):
            owners = list(waiting)
            inv = _pair_series(jnp.concatenate([waiting[o] for o in owners], axis=0), eye2)
            start = 0
            for o in owners:
                n = waiting[o].shape[0]
                inbox[o] = inv[start:start + n]
                start += n
            waiting.clear()


def _mixers(pa, pb, pc, gdn_params, rwkv_params, ret_params):
    bsz, seq, _ = pa.shape
    n_chunks = seq // CHUNK
    chunked = lambda width: pl.BlockSpec((bsz, CHUNK, width), lambda c: (0, (c + n_chunks - 1) % n_chunks, 0))
    rope_spec = pl.BlockSpec((CHUNK, RET_HEADS * RET_DK), lambda c: (c, 0))
    params = tuple(gdn_params) + tuple(rwkv_params) + tuple(ret_params)
    param_specs = [_const_spec(p.shape) for p in gdn_params + rwkv_params] + \
                  [rope_spec, rope_spec] + [_const_spec(p.shape) for p in ret_params[2:]]
    return pl.pallas_call(
        _mixers_kernel,
        grid=(n_chunks,),
        in_specs=[chunked(D_A_COLS), chunked(D_B_IN), chunked(D_C_IN)] + param_specs,
        out_specs=[chunked(D_A), chunked(D_B), chunked(D_C)],
        out_shape=[jax.ShapeDtypeStruct((bsz, seq, D_A), BF16),
                   jax.ShapeDtypeStruct((bsz, seq, D_B), BF16),
                   jax.ShapeDtypeStruct((bsz, seq, D_C), BF16)],
        scratch_shapes=[pltpu.VMEM((bsz, 8 + CHUNK, GDN_QKV), F32),
                        pltpu.VMEM((GDN_HEADS * bsz, GDN_DK, GDN_DV), F32),
                        pltpu.VMEM((bsz, 8 + CHUNK, D_B_IN), F32),
                        pltpu.VMEM((RWKV_HEADS // 2 * bsz, 2 * RWKV_HEAD, 2 * RWKV_HEAD), F32),
                        pltpu.VMEM((RET_HEADS * bsz, RET_DK, RET_DV), F32)],
        compiler_params=pltpu.CompilerParams(dimension_semantics=("arbitrary",),
                                             vmem_limit_bytes=VMEM_LIMIT_BYTES),
        name="mixers",
    )(pa, pb, pc, *params)


def _rope_tables(seq):
    half = RET_DK // 2
    inv_freq = 1.0 / (ROPE_BASE ** jnp.linspace(0.0, 1.0, half, dtype=F32))
    ang = jnp.arange(seq, dtype=F32)[:, None] * inv_freq
    cos, sin = jnp.cos(ang), jnp.sin(ang)
    cos_t = jnp.tile(jnp.concatenate([cos, cos], -1), (1, RET_HEADS))
    sin_t = jnp.tile(jnp.concatenate([-sin, sin], -1), (1, RET_HEADS))
    return cos_t, sin_t


def _row(v):
    return v.reshape(1, -1)


def _pad_gate_vec(v):
    return jnp.zeros((1, GATE_PAD), F32).at[0, GDN_HEADS:2 * GDN_HEADS].set(v)


def _mixer_layer(h3, w_in, gdn, rwkv, ret, rope):
    wa = jnp.concatenate([w_in[:, :D_A_IN], jnp.zeros((D_MODEL, D_A_COLS - D_A_IN), w_in.dtype)], axis=1)
    wb = w_in[:, D_A_IN:D_A_IN + D_B_IN]
    wc = w_in[:, D_A_IN + D_B_IN:]
    pa, pb, pc = _in_proj(h3, wa.astype(BF16), wb.astype(BF16), wc.astype(BF16))
    conv_w, a_log, dt_bias, norm_w = gdn
    return _mixers(
        pa, pb, pc,
        [conv_w, _pad_gate_vec(a_log), _pad_gate_vec(dt_bias), _row(norm_w)],
        [_row(p) if p.ndim == 1 else p for p in rwkv],
        [rope[0], rope[1], _row(ret[0]), _row(ret[1])])


def kernel(x, meta_tokens, ln_g, ln_b, w_ff1_in, w_ff1_out, w_ff2_in, w_ff2_out, w_in, w_out,
           gdn_conv_w, gdn_a_log, gdn_dt_bias, gdn_norm_w, rwkv_mu, rwkv_w0, rwkv_w_up, rwkv_a0,
           rwkv_a_up, rwkv_g_up, rwkv_k_k, rwkv_k_a, rwkv_r_k, rwkv_lnx_g, rwkv_lnx_b,
           ret_norm_g, ret_norm_b):
    bsz, n_real, d = x.shape
    seq = n_real + CHUNK
    depth = ln_g.shape[0]
    rope = _rope_tables(seq)
    meta_chunk = jnp.concatenate([jnp.zeros((META_PAD, d), x.dtype), meta_tokens.astype(x.dtype)], axis=0)[None]
    h = None
    for l in range(depth):
        ffn1 = (w_ff1_in[l].astype(BF16), w_ff1_out[l].astype(BF16), _row(ln_g[l, 0]), _row(ln_b[l, 0]))
        if l == 0:
            h = _ffn_ln(x, *ffn1, out_seq=seq)
            meta_out = _ffn_ln(meta_chunk, *ffn1)
            h = lax.dynamic_update_slice(h, jnp.broadcast_to(meta_out, (bsz, CHUNK, d)), (0, n_real, 0))
        else:
            h = _ffn_ln(h, *ffn1)
        ya, yb, yc = _mixer_layer(
            h, w_in[l],
            (gdn_conv_w[l], gdn_a_log[l], gdn_dt_bias[l], gdn_norm_w[l]),
            (rwkv_mu[l], rwkv_w0[l], rwkv_w_up[l], rwkv_a0[l], rwkv_a_up[l], rwkv_g_up[l],
             rwkv_k_k[l], rwkv_k_a[l], rwkv_r_k[l].reshape(-1), rwkv_lnx_g[l], rwkv_lnx_b[l]),
            (ret_norm_g[l], ret_norm_b[l]), rope)
        h = _mix_ffn_ln(h, ya, yb, yc, w_out[l].astype(BF16), _row(ln_g[l, 1]), _row(ln_b[l, 1]),
                        w_ff2_in[l].astype(BF16), w_ff2_out[l].astype(BF16), _row(ln_g[l, 2]), _row(ln_b[l, 2]),
                        out_seq=n_real if l == depth - 1 else None)
    return h
```

```python
import functools
import math

import jax
import jax.numpy as jnp
from jax import lax
from jax.experimental import pallas as pl
from jax.experimental.pallas import tpu as pltpu

F32 = jnp.float32
BF16 = jnp.bfloat16

D_MODEL = 1024
DEPTH = 2
CHUNK = 64
N_META = 16
META_PAD = CHUNK - N_META
D_FF = 2816
LN_EPS = 1e-5
ALPHA = (2.0 * DEPTH) ** 0.25

GDN_HEADS, GDN_DK, GDN_DV, GDN_CONV = 4, 128, 128, 4
GDN_QKV = 2 * GDN_HEADS * GDN_DK + GDN_HEADS * GDN_DV
D_A = GDN_HEADS * GDN_DV
GATE_PAD = 128
D_A_COLS = GDN_QKV + D_A + GATE_PAD
RWKV_HEADS, RWKV_HEAD = 4, 64
RWKV_DECAY_RANK, RWKV_A_RANK, RWKV_GATE_RANK = 32, 32, 64
RWKV_LNX_EPS = 64e-5
D_B = RWKV_HEADS * RWKV_HEAD
D_B_IN = 3 * D_B + RWKV_DECAY_RANK + RWKV_A_RANK + RWKV_GATE_RANK
RET_HEADS, RET_DK, RET_DV = 4, 32, 64
ROPE_BASE = 10000.0
D_C = RET_HEADS * RET_DV
D_C_IN = 2 * RET_HEADS * RET_DK + 2 * D_C
D_A_IN = GDN_QKV + D_A + 2 * GDN_HEADS

VMEM_LIMIT_BYTES = 56 * 1024 * 1024
ROW_TILES = (1032, 1024, 768, 512, 256, 128, CHUNK)
FF_CHUNKS = ((0, 1024), (1024, 1024), (2048, 768))


def _dot(a, b):
    return jnp.dot(a, b, preferred_element_type=F32)


def _bdot(a, b):
    return _dot(a.astype(BF16), b.astype(BF16))


def _layer_norm(y, g, b):
    mu = jnp.mean(y, axis=-1, keepdims=True)
    d = y - mu
    var = jnp.mean(d * d, axis=-1, keepdims=True)
    return d * lax.rsqrt(var + LN_EPS) * g + b


def _silu(x):
    return x * jax.nn.sigmoid(x)


def _softplus(x):
    return jnp.maximum(x, 0.0) + jnp.log(1.0 + jnp.exp(-jnp.abs(x)))


def _tri_masks():
    row = lax.broadcasted_iota(jnp.int32, (CHUNK, CHUNK), 0)
    col = lax.broadcasted_iota(jnp.int32, (CHUNK, CHUNK), 1)
    return row >= col, row > col, row == col


def _split2(x):
    hi = x.astype(BF16)
    lo = (x - hi.astype(F32)).astype(BF16)
    return hi, lo


def _split3(x):
    hi = x.astype(BF16)
    r1 = x - hi.astype(F32)
    mid = r1.astype(BF16)
    lo = (r1 - mid.astype(F32)).astype(BF16)
    return hi, mid, lo


def _mm3(a, b):
    ah, al = _split2(a)
    bh, bl = _split2(b)
    return _dot(ah, bh) + (_dot(ah, bl) + _dot(al, bh))


def _mm_exact_lhs(a_bf16, pieces):
    out = _dot(a_bf16, pieces[0])
    for p in pieces[1:]:
        out = out + _dot(a_bf16, p)
    return out


def _row_tile(n):
    for cand in ROW_TILES:
        if n % cand == 0:
            return cand
    return n


def _const_spec(shape):
    return pl.BlockSpec(shape, lambda *_: (0,) * len(shape), pipeline_mode=pl.Buffered(1))


def _layer_spec(w_stack, layer):
    _, rows, cols = w_stack.shape
    return pl.BlockSpec((None, rows, cols), lambda *_: (layer, 0, 0), pipeline_mode=pl.Buffered(1))


def _row_spec(tile, width):
    return pl.BlockSpec((1, tile, width), lambda b, j: (b, j, 0))


def _swiglu_ln(x, win_ref, wout_ref, g_ref, b_ref):
    xb = x.astype(BF16)
    acc = jnp.zeros(x.shape, F32)
    for start, width in FF_CHUNKS:
        gate = _dot(xb, win_ref[:, start:start + width])
        up = _dot(xb, win_ref[:, D_FF + start:D_FF + start + width])
        act = (_silu(gate) * up).astype(BF16)
        acc = acc + _dot(act, wout_ref[start:start + width, :])
    return _layer_norm(ALPHA * x + 0.5 * acc, g_ref[...], b_ref[...])


def _ffn_ln_kernel(x_ref, win_ref, wout_ref, g_ref, b_ref, o_ref):
    o_ref[0] = _swiglu_ln(x_ref[0], win_ref, wout_ref, g_ref, b_ref)


def _ffn_ln(h3, w_in, w_out, g, b, layer):
    bsz, seq, _ = h3.shape
    tile = _row_tile(seq)
    return pl.pallas_call(
        _ffn_ln_kernel,
        grid=(bsz, seq // tile),
        in_specs=[_row_spec(tile, D_MODEL),
                  _layer_spec(w_in, layer),
                  _layer_spec(w_out, layer),
                  _const_spec((1, D_MODEL)),
                  _const_spec((1, D_MODEL))],
        out_specs=_row_spec(tile, D_MODEL),
        out_shape=jax.ShapeDtypeStruct((bsz, seq, D_MODEL), F32),
        compiler_params=pltpu.CompilerParams(dimension_semantics=("parallel", "parallel"),
                                             vmem_limit_bytes=VMEM_LIMIT_BYTES),
        name="ffn_ln",
    )(h3, w_in, w_out, g, b)


def _first_ffn_ln_kernel(x_ref, meta_ref, win_ref, wout_ref, g_ref, b_ref, o_ref, *, n_main):
    j = pl.program_id(1)

    @pl.when(j < n_main)
    def _():
        o_ref[0] = _swiglu_ln(x_ref[0], win_ref, wout_ref, g_ref, b_ref)

    @pl.when(j == n_main)
    def _():
        o_ref[0, 0:CHUNK, :] = _swiglu_ln(meta_ref[...], win_ref, wout_ref, g_ref, b_ref)


def _first_ffn_ln(x, meta_chunk, w_in, w_out, g, b, layer):
    bsz, n_real, _ = x.shape
    tile = _row_tile(n_real)
    n_main = n_real // tile
    return pl.pallas_call(
        functools.partial(_first_ffn_ln_kernel, n_main=n_main),
        grid=(bsz, n_main + 1),
        in_specs=[pl.BlockSpec((1, tile, D_MODEL), lambda bi, j: (bi, jnp.minimum(j, n_main - 1), 0)),
                  _const_spec((CHUNK, D_MODEL)),
                  _layer_spec(w_in, layer),
                  _layer_spec(w_out, layer),
                  _const_spec((1, D_MODEL)),
                  _const_spec((1, D_MODEL))],
        out_specs=_row_spec(tile, D_MODEL),
        out_shape=jax.ShapeDtypeStruct((bsz, n_real + CHUNK, D_MODEL), F32),
        compiler_params=pltpu.CompilerParams(dimension_semantics=("parallel", "arbitrary"),
                                             vmem_limit_bytes=VMEM_LIMIT_BYTES),
        name="first_ffn_ln",
    )(x, meta_chunk, w_in, w_out, g, b)


def _in_proj_kernel(h_ref, wa_ref, wb_ref, wc_ref, pa_ref, pb_ref, pc_ref, *, tile, seq):
    pos = pl.program_id(1) * tile + lax.broadcasted_iota(jnp.int32, (tile, 1), 0)
    inert = (pos >= seq - CHUNK) & (pos < seq - N_META)
    hb = jnp.where(inert, 0.0, h_ref[0]).astype(BF16)
    pa_ref[0] = _dot(hb, wa_ref[...])
    pb_ref[0] = _dot(hb, wb_ref[...])
    pc_ref[0] = _dot(hb, wc_ref[...])


def _in_proj(h, wa, wb, wc):
    bsz, seq, _ = h.shape
    tile = _row_tile(seq)
    return pl.pallas_call(
        functools.partial(_in_proj_kernel, tile=tile, seq=seq),
        grid=(bsz, seq // tile),
        in_specs=[_row_spec(tile, D_MODEL), _const_spec((D_MODEL, D_A_COLS)),
                  _const_spec((D_MODEL, D_B_IN)), _const_spec((D_MODEL, D_C_IN))],
        out_specs=[_row_spec(tile, D_A_COLS), _row_spec(tile, D_B_IN), _row_spec(tile, D_C_IN)],
        out_shape=[jax.ShapeDtypeStruct((bsz, seq, D_A_COLS), F32),
                   jax.ShapeDtypeStruct((bsz, seq, D_B_IN), F32),
                   jax.ShapeDtypeStruct((bsz, seq, D_C_IN), F32)],
        compiler_params=pltpu.CompilerParams(dimension_semantics=("parallel", "parallel"),
                                             vmem_limit_bytes=VMEM_LIMIT_BYTES),
        name="in_proj",
    )(h, wa, wb, wc)


def _mix_ffn_ln_kernel(h_ref, ya_ref, yb_ref, yc_ref, wmix_ref, g1_ref, b1_ref, win_ref, wout_ref,
                       g2_ref, b2_ref, o_ref):
    mix = _dot(ya_ref[0], wmix_ref[0:D_A, :])
    mix = mix + _dot(yb_ref[0], wmix_ref[D_A:D_A + D_B, :])
    mix = mix + _dot(yc_ref[0], wmix_ref[D_A + D_B:, :])
    h2 = _layer_norm(ALPHA * h_ref[0] + mix, g1_ref[...], b1_ref[...])
    o_ref[0] = _swiglu_ln(h2, win_ref, wout_ref, g2_ref, b2_ref)


def _mix_ffn_ln(h, ya, yb, yc, w_mix, g1, b1, w_in, w_out, g2, b2, layer, out_seq=None):
    bsz, seq, _ = h.shape
    out_seq = seq if out_seq is None else out_seq
    tile = _row_tile(min(seq, out_seq))
    vec = _const_spec((1, D_MODEL))
    return pl.pallas_call(
        _mix_ffn_ln_kernel,
        grid=(bsz, min(seq, out_seq) // tile),
        in_specs=[_row_spec(tile, D_MODEL), _row_spec(tile, D_A), _row_spec(tile, D_B), _row_spec(tile, D_C),
                  _layer_spec(w_mix, layer), vec, vec,
                  _layer_spec(w_in, layer), _layer_spec(w_out, layer), vec, vec],
        out_specs=_row_spec(tile, D_MODEL),
        out_shape=jax.ShapeDtypeStruct((bsz, out_seq, D_MODEL), F32),
        compiler_params=pltpu.CompilerParams(dimension_semantics=("parallel", "parallel"),
                                             vmem_limit_bytes=VMEM_LIMIT_BYTES),
        name="mix_ffn_ln",
    )(h, ya, yb, yc, w_mix, g1, b1, w_in, w_out, g2, b2)


def _bmm(a, b):
    return jnp.einsum('nij,njk->nik', a, b, preferred_element_type=F32)


def _bmm_nt(a, b):
    return jnp.einsum('nik,njk->nij', a, b, preferred_element_type=F32)


def _bmm_tn(a, b):
    return jnp.einsum('nki,nkj->nij', a, b, preferred_element_type=F32)


def _b1(mm, a, b):
    return mm(a.astype(BF16), b.astype(BF16))


def _pair_diag(y):
    left = lax.broadcasted_iota(jnp.int32, (1, 1, 2 * RWKV_HEAD), 2) < RWKV_HEAD
    return jnp.concatenate([jnp.where(left, y, 0.0), jnp.where(left, 0.0, y)], axis=1)


def _pair_series(x, eye):
    p = eye + x
    xd = _pair_diag(x).astype(BF16)
    for _ in range(5):
        x = _bmm(x.astype(BF16), xd)
        xd = _pair_diag(x).astype(BF16)
        p = p + _bmm(p.astype(BF16), xd)
    return p


def _to_chains(x3, heads, width, offset=0):
    return jnp.concatenate([x3[:, :, offset + h * width:offset + (h + 1) * width] for h in range(heads)], axis=0)


def _from_chains(xn, heads):
    bsz = xn.shape[0] // heads
    return jnp.concatenate([xn[h * bsz:(h + 1) * bsz] for h in range(heads)], axis=2)


def _block_tril(rows):
    r = lax.broadcasted_iota(jnp.int32, (rows, rows), 0)
    c = lax.broadcasted_iota(jnp.int32, (rows, rows), 1)
    return ((r // CHUNK == c // CHUNK) & (r >= c)).astype(BF16)


def _head_group_matrix(width, head, value):
    r = lax.broadcasted_iota(jnp.int32, (width, width), 0) // head
    c = lax.broadcasted_iota(jnp.int32, (width, width), 1) // head
    return jnp.where(r == c, value, 0.0).astype(BF16)


def _gdn_body(pa_ref, conv_ref, alog_ref, dt_ref, nw_ref, o_ref, cbuf, state):
    bsz = pa_ref.shape[0]
    rows = bsz * CHUNK
    causal, strict, _ = _tri_masks()

    x = pa_ref[:, :, 0:GDN_QKV]
    cbuf[:, 8:8 + CHUNK, :] = x
    y = conv_ref[3:4, :] * x
    for tap in range(GDN_CONV - 1):
        lo = 8 - (GDN_CONV - 1) + tap
        y = y + conv_ref[tap:tap + 1, :] * cbuf[:, lo:lo + CHUNK, :]
    cbuf[:, 0:8, :] = x[:, CHUNK - 8:, :]
    qkv = _silu(y)

    yield
    gates = pa_ref[:, :, GDN_QKV + D_A:].reshape(rows, GATE_PAD)
    beta_all = jax.nn.sigmoid(gates).reshape(bsz, CHUNK, GATE_PAD)
    lg_all = -jnp.exp(alog_ref[...]) * _softplus(gates + dt_ref[...])
    g_all = _mm_exact_lhs(_block_tril(rows), _split3(lg_all))
    g_t = g_all.T
    g_all = g_all.reshape(bsz, CHUNK, GATE_PAD)
    beta = jnp.concatenate([beta_all[:, :, h:h + 1] for h in range(GDN_HEADS)], axis=0)
    gc = jnp.concatenate([g_all[:, :, GDN_HEADS + h:GDN_HEADS + h + 1] for h in range(GDN_HEADS)], axis=0)
    g_row = jnp.concatenate([g_t[GDN_HEADS + h:GDN_HEADS + h + 1, b * CHUNK:(b + 1) * CHUNK][None]
                             for h in range(GDN_HEADS) for b in range(bsz)], axis=0)

    yield
    q = _to_chains(qkv, GDN_HEADS, GDN_DK)
    k = _to_chains(qkv, GDN_HEADS, GDN_DK, GDN_HEADS * GDN_DK)
    v = _to_chains(qkv, GDN_HEADS, GDN_DV, 2 * GDN_HEADS * GDN_DK)
    q = q * (lax.rsqrt(jnp.sum(q * q, -1, keepdims=True) + 1e-6) * GDN_DK ** -0.5)
    k = k * lax.rsqrt(jnp.sum(k * k, -1, keepdims=True) + 1e-6)
    diff = gc - g_row
    decay = jnp.where(causal, jnp.exp(jnp.where(causal, diff, 0.0)), 0.0)
    yield
    kb = k * beta
    kq = _b1(_bmm_nt, jnp.concatenate([kb, q], axis=1), k)
    a_mat = jnp.where(strict, kq[:, :CHUNK] * decay, 0.0)
    qk = kq[:, CHUNK:] * decay
    half = a_mat.shape[0] // 2
    packed = yield jnp.concatenate([-a_mat[:half], -a_mat[half:]], axis=2)
    t_inv = jnp.concatenate([packed[:, :, :CHUNK], packed[:, :, CHUNK:]], axis=0)
    e_g = jnp.exp(gc)
    sol = _b1(_bmm, t_inv, jnp.concatenate([v * beta, kb * e_g], axis=2))
    u, w = sol[:, :, :GDN_DV], sol[:, :, GDN_DV:]
    g_last = gc[:, CHUNK - 1:CHUNK, :]
    k_dec = k * jnp.exp(g_last - gc)
    s0 = state[...]
    ws = _b1(_bmm, jnp.concatenate([w, q * e_g], axis=1), s0)
    v_new = u - ws[:, :CHUNK]
    o = ws[:, CHUNK:] + _b1(_bmm, qk, v_new)
    state[...] = s0 * jnp.exp(g_last) + _b1(_bmm_tn, k_dec, v_new)
    yield
    o = o * lax.rsqrt(jnp.mean(o * o, -1, keepdims=True) + LN_EPS) * nw_ref[...]
    z = _to_chains(pa_ref[:, :, GDN_QKV:GDN_QKV + D_A], GDN_HEADS, GDN_DV)
    o_ref[...] = _from_chains(o * _silu(z), GDN_HEADS).astype(o_ref.dtype)


def _rwkv_body(pb_ref, mu_ref, w0_ref, wup_ref, a0_ref, aup_ref, gup_ref, kk_ref, ka_ref,
               rk_ref, lg_ref, lb_ref, o_ref, sbuf, state):
    bsz = pb_ref.shape[0]
    rows = bsz * CHUNK
    head_ones = _head_group_matrix(D_B, RWKV_HEAD, 1.0)
    head_avg = _head_group_matrix(D_B, RWKV_HEAD, 1.0 / RWKV_HEAD)

    x = pb_ref[...]
    sbuf[:, 8:8 + CHUNK, :] = x
    prev = sbuf[:, 7:7 + CHUNK, :]
    sbuf[:, 0:8, :] = x[:, CHUNK - 8:, :]
    pf = (x + (prev - x) * mu_ref[...]).reshape(rows, D_B_IN)
    r_all, k_all, v_all = pf[:, 0:D_B], pf[:, D_B:2 * D_B], pf[:, 2 * D_B:3 * D_B]
    c0 = 3 * D_B
    xw = pf[:, c0:c0 + RWKV_DECAY_RANK]
    xa = pf[:, c0 + RWKV_DECAY_RANK:c0 + RWKV_DECAY_RANK + RWKV_A_RANK]
    xg = pf[:, c0 + RWKV_DECAY_RANK + RWKV_A_RANK:]
    log_w = -_softplus(-(w0_ref[...] + _mm3(jnp.tanh(xw), wup_ref[...]))) - 0.5
    wlog = -jnp.exp(log_w)
    a_all = jax.nn.sigmoid(a0_ref[...] + _mm3(xa, aup_ref[...]))
    gate_all = _bdot(jax.nn.sigmoid(xg), gup_ref[...])
    yield
    kk = k_all * kk_ref[...]
    kk = kk * lax.rsqrt(_bdot(kk * kk, head_ones) + 1e-6)
    k2 = k_all * (1.0 + (a_all - 1.0) * ka_ref[...])
    bb = kk * a_all
    gc = _mm_exact_lhs(_block_tril(rows), _split2(wlog))
    e_neg = jnp.exp(-gc)
    r_t = r_all * jnp.exp(gc)
    a_t = -kk * jnp.exp(gc - wlog)
    k_t = k2 * e_neg
    b_t = bb * e_neg
    gc3 = gc.reshape(bsz, CHUNK, D_B)
    g_last = gc3[:, CHUNK - 1:CHUNK, :]
    e_tail = jnp.exp(g_last - gc3).reshape(rows, D_B)
    bonus = _bdot(r_all * k2 * rk_ref[...], head_ones) * v_all

    yield
    pair_w = 2 * RWKV_HEAD
    n_pairs = RWKV_HEADS // 2
    pairs = lambda t: _to_chains(t.reshape(bsz, CHUNK, D_B), n_pairs, pair_w)
    a_n, r_n, b_n, k_n, v_n = pairs(a_t), pairs(r_t), pairs(b_t), pairs(k_t), pairs(v_all)
    bdec_n, kdec_n = pairs(bb * e_tail), pairs(k2 * e_tail)
    elast_n = _to_chains(jnp.exp(g_last), n_pairs, pair_w)
    prow = lax.broadcasted_iota(jnp.int32, (CHUNK, pair_w), 0)
    pcol = lax.broadcasted_iota(jnp.int32, (CHUNK, pair_w), 1) % RWKV_HEAD
    causal2, strict2, eye2 = prow >= pcol, prow > pcol, (prow == pcol).astype(F32)
    srow = lax.broadcasted_iota(jnp.int32, (pair_w, pair_w), 0) // RWKV_HEAD
    scol = lax.broadcasted_iota(jnp.int32, (pair_w, pair_w), 1) // RWKV_HEAD
    same_head = srow == scol

    ar = jnp.concatenate([a_n, r_n], axis=1).astype(BF16)
    xb = _bmm_nt(ar, _pair_diag(b_n).astype(BF16))
    xk = _bmm_nt(ar, _pair_diag(k_n).astype(BF16))
    a_ab = jnp.where(strict2, xb[:, :CHUNK], 0.0)
    a_rb = jnp.where(causal2, xb[:, CHUNK:], 0.0)
    a_k = jnp.concatenate([jnp.where(strict2, xk[:, :CHUNK], 0.0), jnp.where(causal2, xk[:, CHUNK:], 0.0)], axis=1)
    m_inv = (yield a_ab).astype(BF16)
    akv = _bmm(a_k.astype(BF16), _pair_diag(v_n).astype(BF16))
    w_mat = _bmm(m_inv, _pair_diag(a_n).astype(BF16))
    u2 = _bmm(m_inv, _pair_diag(akv[:, :CHUNK]).astype(BF16))
    s0 = state[...]
    wr = _b1(_bmm_nt, jnp.concatenate([w_mat, r_n], axis=1), s0)
    u = wr[:, :CHUNK] + u2
    y_n = wr[:, CHUNK:] + _b1(_bmm, a_rb, _pair_diag(u)) + akv[:, CHUNK:]
    upd = _b1(_bmm_tn, jnp.concatenate([u, v_n], axis=1), jnp.concatenate([bdec_n, kdec_n], axis=1))
    state[...] = s0 * elast_n + jnp.where(same_head, upd, 0.0)

    yield
    y = _from_chains(y_n, n_pairs).reshape(rows, D_B)
    mu = _bdot(y, head_avg)
    d = y - mu
    var = _bdot(d * d, head_avg)
    yn = d * lax.rsqrt(var + RWKV_LNX_EPS) * lg_ref[...] + lb_ref[...]
    o_ref[...] = ((yn + bonus) * gate_all).reshape(bsz, CHUNK, D_B).astype(o_ref.dtype)


def _ret_log_gamma(h):
    return math.log(1.0 - 2.0 ** (-5.0 - h))


def _ret_body(pc_ref, cos_ref, sin_ref, ng_ref, nb_ref, o_ref, state):
    bsz = pc_ref.shape[0]
    rows = bsz * CHUNK
    qk_w = RET_HEADS * RET_DK
    lane = lax.broadcasted_iota(jnp.int32, (bsz, CHUNK, qk_w), 2)
    first_half = (lane % RET_DK) < (RET_DK // 2)

    def rope(t):
        partner = jnp.where(first_half, pltpu.roll(t, qk_w - RET_DK // 2, 2), pltpu.roll(t, RET_DK // 2, 2))
        return t * cos_ref[...] + partner * sin_ref[...]

    row = lax.broadcasted_iota(jnp.int32, (CHUNK, CHUNK), 0)
    col = lax.broadcasted_iota(jnp.int32, (CHUNK, CHUNK), 1)
    delta = (row - col).astype(F32)
    idx = lax.broadcasted_iota(jnp.int32, (CHUNK, 1), 0).astype(F32)
    head_avg = _head_group_matrix(D_C, RET_DV, 1.0 / RET_DV)
    rep = lambda t: jnp.broadcast_to(t[None], (bsz,) + t.shape)
    d_intra = jnp.concatenate(
        [rep(jnp.where(delta >= 0, jnp.exp(_ret_log_gamma(h) * jnp.maximum(delta, 0.0)), 0.0))
         for h in range(RET_HEADS)], axis=0)
    q_scale = jnp.concatenate([rep(jnp.exp(_ret_log_gamma(h) * (idx + 1.0))) for h in range(RET_HEADS)], axis=0)
    k_scale = jnp.concatenate([rep(jnp.exp(_ret_log_gamma(h) * (CHUNK - 1.0 - idx))) for h in range(RET_HEADS)],
                              axis=0)
    s_scale = jnp.concatenate([jnp.full((bsz, 1, 1), math.exp(_ret_log_gamma(h) * CHUNK), F32)
                               for h in range(RET_HEADS)], axis=0)

    yield
    q = _to_chains(rope(pc_ref[:, :, 0:qk_w]), RET_HEADS, RET_DK)
    k = _to_chains(rope(pc_ref[:, :, qk_w:2 * qk_w]) * RET_DK ** -0.5, RET_HEADS, RET_DK)
    v = _to_chains(pc_ref[:, :, 2 * qk_w:2 * qk_w + D_C], RET_HEADS, RET_DV)
    intra = _b1(_bmm, _b1(_bmm_nt, q, k) * d_intra, v)
    s0 = state[...]
    y_n = intra + _b1(_bmm, q * q_scale, s0)
    state[...] = s0 * s_scale + _b1(_bmm_tn, k * k_scale, v)
    yield
    y = _from_chains(y_n, RET_HEADS).reshape(rows, D_C)
    mu = _bdot(y, head_avg)
    d = y - mu
    var = _bdot(d * d, head_avg)
    yn = d * lax.rsqrt(var + LN_EPS) * ng_ref[...] + nb_ref[...]
    gate = pc_ref[:, :, 2 * qk_w + D_C:].reshape(rows, D_C)
    o_ref[...] = (yn * _silu(gate)).reshape(bsz, CHUNK, D_C).astype(o_ref.dtype)


N_GDN_PARAMS, N_RWKV_PARAMS, N_RET_PARAMS = 4, 11, 4


def _mixers_kernel(*refs):
    pa_ref, pb_ref, pc_ref = refs[:3]
    params = refs[3:3 + N_GDN_PARAMS + N_RWKV_PARAMS + N_RET_PARAMS]
    gdn_p, rwkv_p = params[:N_GDN_PARAMS], params[N_GDN_PARAMS:N_GDN_PARAMS + N_RWKV_PARAMS]
    ret_p = params[N_GDN_PARAMS + N_RWKV_PARAMS:]
    oa_ref, ob_ref, oc_ref, cbuf, gdn_state, sbuf, rwkv_state, ret_state = refs[3 + len(params):]

    @pl.when(pl.program_id(0) == 0)
    def _():
        cbuf[:, 0:8, :] = jnp.zeros((cbuf.shape[0], 8, GDN_QKV), F32)
        sbuf[:, 0:8, :] = jnp.zeros((sbuf.shape[0], 8, D_B_IN), F32)
        gdn_state[...] = jnp.zeros(gdn_state.shape, F32)
        rwkv_state[...] = jnp.zeros(rwkv_state.shape, F32)
        ret_state[...] = jnp.zeros(ret_state.shape, F32)

    bodies = [_gdn_body(pa_ref, *gdn_p, oa_ref, cbuf, gdn_state),
              _rwkv_body(pb_ref, *rwkv_p, ob_ref, sbuf, rwkv_state),
              _ret_body(pc_ref, *ret_p, oc_ref, ret_state)]
    prow = lax.broadcasted_iota(jnp.int32, (CHUNK, 2 * CHUNK), 0)
    pcol = lax.broadcasted_iota(jnp.int32, (CHUNK, 2 * CHUNK), 1) % CHUNK
    eye2 = (prow == pcol).astype(F32)
    inbox = {body: None for body in bodies}
    waiting = {}
    while bodies:
        for body in list(bodies):
            if body in waiting:
                continue
            try:
                request = body.send(inbox[body])
            except StopIteration:
                bodies.remove(body)
                continue
            inbox[body] = None
            if request is not None:
                waiting[body] = request
        if len(waiting) == 2 or (waiting and all(body in waiting for body in bodies)):
            owners = list(waiting)
            inv = _pair_series(jnp.concatenate([waiting[o] for o in owners], axis=0), eye2)
            start = 0
            for o in owners:
                n = waiting[o].shape[0]
                inbox[o] = inv[start:start + n]
                start += n
            waiting.clear()


def _mixers(pa, pb, pc, gdn_params, rwkv_params, ret_params):
    bsz, seq, _ = pa.shape
    n_chunks = seq // CHUNK
    chunked = lambda width: pl.BlockSpec((bsz, CHUNK, width), lambda c: (0, (c + n_chunks - 1) % n_chunks, 0))
    rope_spec = pl.BlockSpec((CHUNK, RET_HEADS * RET_DK), lambda c: (c, 0))
    params = tuple(gdn_params) + tuple(rwkv_params) + tuple(ret_params)
    param_specs = [_const_spec(p.shape) for p in gdn_params + rwkv_params] + \
                  [rope_spec, rope_spec] + [_const_spec(p.shape) for p in ret_params[2:]]
    return pl.pallas_call(
        _mixers_kernel,
        grid=(n_chunks,),
        in_specs=[chunked(D_A_COLS), chunked(D_B_IN), chunked(D_C_IN)] + param_specs,
        out_specs=[chunked(D_A), chunked(D_B), chunked(D_C)],
        out_shape=[jax.ShapeDtypeStruct((bsz, seq, D_A), BF16),
                   jax.ShapeDtypeStruct((bsz, seq, D_B), BF16),
                   jax.ShapeDtypeStruct((bsz, seq, D_C), BF16)],
        scratch_shapes=[pltpu.VMEM((bsz, 8 + CHUNK, GDN_QKV), F32),
                        pltpu.VMEM((GDN_HEADS * bsz, GDN_DK, GDN_DV), F32),
                        pltpu.VMEM((bsz, 8 + CHUNK, D_B_IN), F32),
                        pltpu.VMEM((RWKV_HEADS // 2 * bsz, 2 * RWKV_HEAD, 2 * RWKV_HEAD), F32),
                        pltpu.VMEM((RET_HEADS * bsz, RET_DK, RET_DV), F32)],
        compiler_params=pltpu.CompilerParams(dimension_semantics=("arbitrary",),
                                             vmem_limit_bytes=VMEM_LIMIT_BYTES),
        name="mixers",
    )(pa, pb, pc, *params)


def _rope_tables(seq):
    half = RET_DK // 2
    inv_freq = 1.0 / (ROPE_BASE ** jnp.linspace(0.0, 1.0, half, dtype=F32))
    ang = jnp.arange(seq, dtype=F32)[:, None] * inv_freq
    cos, sin = jnp.cos(ang), jnp.sin(ang)
    cos_t = jnp.tile(jnp.concatenate([cos, cos], -1), (1, RET_HEADS))
    sin_t = jnp.tile(jnp.concatenate([-sin, sin], -1), (1, RET_HEADS))
    return cos_t, sin_t


def _row(v):
    return v.reshape(1, -1)


def _pad_gate_vec(v):
    return jnp.zeros((1, GATE_PAD), F32).at[0, GDN_HEADS:2 * GDN_HEADS].set(v)


def _mixer_layer(h3, w_in, gdn, rwkv, ret, rope):
    wa = jnp.concatenate([w_in[:, :D_A_IN], jnp.zeros((D_MODEL, D_A_COLS - D_A_IN), w_in.dtype)], axis=1)
    wb = w_in[:, D_A_IN:D_A_IN + D_B_IN]
    wc = w_in[:, D_A_IN + D_B_IN:]
    pa, pb, pc = _in_proj(h3, wa.astype(BF16), wb.astype(BF16), wc.astype(BF16))
    conv_w, a_log, dt_bias, norm_w = gdn
    return _mixers(
        pa, pb, pc,
        [conv_w, _pad_gate_vec(a_log), _pad_gate_vec(dt_bias), _row(norm_w)],
        [_row(p) if p.ndim == 1 else p for p in rwkv],
        [rope[0], rope[1], _row(ret[0]), _row(ret[1])])


def kernel(x, meta_tokens, ln_g, ln_b, w_ff1_in, w_ff1_out, w_ff2_in, w_ff2_out, w_in, w_out,
           gdn_conv_w, gdn_a_log, gdn_dt_bias, gdn_norm_w, rwkv_mu, rwkv_w0, rwkv_w_up, rwkv_a0,
           rwkv_a_up, rwkv_g_up, rwkv_k_k, rwkv_k_a, rwkv_r_k, rwkv_lnx_g, rwkv_lnx_b,
           ret_norm_g, ret_norm_b):
    bsz, n_real, d = x.shape
    seq = n_real + CHUNK
    depth = ln_g.shape[0]
    rope = _rope_tables(seq)
    meta_chunk = jnp.concatenate([jnp.zeros((META_PAD, d), x.dtype), meta_tokens.astype(x.dtype)], axis=0)
    ff1_in, ff1_out, ff2_in, ff2_out, w_mix = (w.astype(BF16) for w in (w_ff1_in, w_ff1_out, w_ff2_in, w_ff2_out, w_out))
    h = None
    for l in range(depth):
        ffn1 = (ff1_in, ff1_out, _row(ln_g[l, 0]), _row(ln_b[l, 0]), l)
        if l == 0:
            h = _first_ffn_ln(x, meta_chunk, *ffn1)
        else:
            h = _ffn_ln(h, *ffn1)
        ya, yb, yc = _mixer_layer(
            h, w_in[l],
            (gdn_conv_w[l], gdn_a_log[l], gdn_dt_bias[l], gdn_norm_w[l]),
            (rwkv_mu[l], rwkv_w0[l], rwkv_w_up[l], rwkv_a0[l], rwkv_a_up[l], rwkv_g_up[l],
             rwkv_k_k[l], rwkv_k_a[l], rwkv_r_k[l].reshape(-1), rwkv_lnx_g[l], rwkv_lnx_b[l]),
            (ret_norm_g[l], ret_norm_b[l]), rope)
        h = _mix_ffn_ln(h, ya, yb, yc, w_mix, _row(ln_g[l, 1]), _row(ln_b[l, 1]),
                        ff2_in, ff2_out, _row(ln_g[l, 2]), _row(ln_b[l, 2]), l,
                        out_seq=n_real if l == depth - 1 else None)
    return h
```

```python
import functools
import math

import jax
import jax.numpy as jnp
from jax import lax
from jax.experimental import pallas as pl
from jax.experimental.pallas import tpu as pltpu

F32 = jnp.float32
BF16 = jnp.bfloat16

D_MODEL = 1024
DEPTH = 2
CHUNK = 64
N_META = 16
META_PAD = CHUNK - N_META
D_FF = 2816
LN_EPS = 1e-5
ALPHA = (2.0 * DEPTH) ** 0.25

GDN_HEADS, GDN_DK, GDN_DV, GDN_CONV = 4, 128, 128, 4
GDN_QKV = 2 * GDN_HEADS * GDN_DK + GDN_HEADS * GDN_DV
D_A = GDN_HEADS * GDN_DV
GATE_PAD = 128
D_A_COLS = GDN_QKV + D_A + GATE_PAD
RWKV_HEADS, RWKV_HEAD = 4, 64
RWKV_DECAY_RANK, RWKV_A_RANK, RWKV_GATE_RANK = 32, 32, 64
RWKV_LNX_EPS = 64e-5
D_B = RWKV_HEADS * RWKV_HEAD
D_B_IN = 3 * D_B + RWKV_DECAY_RANK + RWKV_A_RANK + RWKV_GATE_RANK
RET_HEADS, RET_DK, RET_DV = 4, 32, 64
ROPE_BASE = 10000.0
D_C = RET_HEADS * RET_DV
D_C_IN = 2 * RET_HEADS * RET_DK + 2 * D_C
D_A_IN = GDN_QKV + D_A + 2 * GDN_HEADS

VMEM_LIMIT_BYTES = 56 * 1024 * 1024
ROW_TILES = (1032, 1024, 768, 512, 256, 128, CHUNK)
FF_CHUNKS = ((0, 1024), (1024, 1024), (2048, 768))


def _dot(a, b):
    return jnp.dot(a, b, preferred_element_type=F32)


def _bdot(a, b):
    return _dot(a.astype(BF16), b.astype(BF16))


def _layer_norm(y, g, b):
    mu = jnp.mean(y, axis=-1, keepdims=True)
    d = y - mu
    var = jnp.mean(d * d, axis=-1, keepdims=True)
    return d * lax.rsqrt(var + LN_EPS) * g + b


def _silu(x):
    return x * jax.nn.sigmoid(x)


def _softplus(x):
    return jnp.maximum(x, 0.0) + jnp.log(1.0 + jnp.exp(-jnp.abs(x)))


def _tri_masks():
    row = lax.broadcasted_iota(jnp.int32, (CHUNK, CHUNK), 0)
    col = lax.broadcasted_iota(jnp.int32, (CHUNK, CHUNK), 1)
    return row >= col, row > col, row == col


def _split2(x):
    hi = x.astype(BF16)
    lo = (x - hi.astype(F32)).astype(BF16)
    return hi, lo


def _split3(x):
    hi = x.astype(BF16)
    r1 = x - hi.astype(F32)
    mid = r1.astype(BF16)
    lo = (r1 - mid.astype(F32)).astype(BF16)
    return hi, mid, lo


def _mm3(a, b):
    ah, al = _split2(a)
    bh, bl = _split2(b)
    return _dot(ah, bh) + (_dot(ah, bl) + _dot(al, bh))


def _mm_exact_lhs(a_bf16, pieces):
    out = _dot(a_bf16, pieces[0])
    for p in pieces[1:]:
        out = out + _dot(a_bf16, p)
    return out


def _row_tile(n):
    for cand in ROW_TILES:
        if n % cand == 0:
            return cand
    return n


def _const_spec(shape):
    return pl.BlockSpec(shape, lambda *_: (0,) * len(shape), pipeline_mode=pl.Buffered(1))


def _layer_spec(w_stack, layer):
    _, rows, cols = w_stack.shape
    return pl.BlockSpec((None, rows, cols), lambda *_: (layer, 0, 0), pipeline_mode=pl.Buffered(1))


def _row_spec(tile, width):
    return pl.BlockSpec((1, tile, width), lambda b, j: (b, j, 0))


def _swiglu_ln(x, win_ref, wout_ref, g_ref, b_ref):
    xb = x.astype(BF16)
    acc = jnp.zeros(x.shape, F32)
    for start, width in FF_CHUNKS:
        gate = _dot(xb, win_ref[:, start:start + width])
        up = _dot(xb, win_ref[:, D_FF + start:D_FF + start + width])
        act = (_silu(gate) * up).astype(BF16)
        acc = acc + _dot(act, wout_ref[start:start + width, :])
    return _layer_norm(ALPHA * x + 0.5 * acc, g_ref[...], b_ref[...])


def _ffn_ln_kernel(x_ref, win_ref, wout_ref, g_ref, b_ref, o_ref):
    o_ref[0] = _swiglu_ln(x_ref[0], win_ref, wout_ref, g_ref, b_ref)


def _ffn_ln(h3, w_in, w_out, g, b, layer):
    bsz, seq, _ = h3.shape
    tile = _row_tile(seq)
    return pl.pallas_call(
        _ffn_ln_kernel,
        grid=(bsz, seq // tile),
        in_specs=[_row_spec(tile, D_MODEL),
                  _layer_spec(w_in, layer),
                  _layer_spec(w_out, layer),
                  _const_spec((1, D_MODEL)),
                  _const_spec((1, D_MODEL))],
        out_specs=_row_spec(tile, D_MODEL),
        out_shape=jax.ShapeDtypeStruct((bsz, seq, D_MODEL), F32),
        compiler_params=pltpu.CompilerParams(dimension_semantics=("parallel", "parallel"),
                                             vmem_limit_bytes=VMEM_LIMIT_BYTES),
        name="ffn_ln",
    )(h3, w_in, w_out, g, b)


def _first_ffn_ln_kernel(x_ref, meta_ref, win_ref, wout_ref, g_ref, b_ref, o_ref, *, n_main):
    j = pl.program_id(1)

    @pl.when(j < n_main)
    def _():
        o_ref[0] = _swiglu_ln(x_ref[0], win_ref, wout_ref, g_ref, b_ref)

    @pl.when(j == n_main)
    def _():
        o_ref[0, 0:CHUNK, :] = _swiglu_ln(meta_ref[...], win_ref, wout_ref, g_ref, b_ref)


def _first_ffn_ln(x, meta_chunk, w_in, w_out, g, b, layer):
    bsz, n_real, _ = x.shape
    tile = _row_tile(n_real)
    n_main = n_real // tile
    return pl.pallas_call(
        functools.partial(_first_ffn_ln_kernel, n_main=n_main),
        grid=(bsz, n_main + 1),
        in_specs=[pl.BlockSpec((1, tile, D_MODEL), lambda bi, j: (bi, jnp.minimum(j, n_main - 1), 0)),
                  _const_spec((CHUNK, D_MODEL)),
                  _layer_spec(w_in, layer),
                  _layer_spec(w_out, layer),
                  _const_spec((1, D_MODEL)),
                  _const_spec((1, D_MODEL))],
        out_specs=_row_spec(tile, D_MODEL),
        out_shape=jax.ShapeDtypeStruct((bsz, n_real + CHUNK, D_MODEL), F32),
        compiler_params=pltpu.CompilerParams(dimension_semantics=("parallel", "arbitrary"),
                                             vmem_limit_bytes=VMEM_LIMIT_BYTES),
        name="first_ffn_ln",
    )(x, meta_chunk, w_in, w_out, g, b)


def _in_proj_kernel(h_ref, wa_ref, wb_ref, wc_ref, pa_ref, pb_ref, pc_ref, *, tile, seq):
    pos = pl.program_id(1) * tile + lax.broadcasted_iota(jnp.int32, (tile, 1), 0)
    inert = (pos >= seq - CHUNK) & (pos < seq - N_META)
    hb = jnp.where(inert, 0.0, h_ref[0]).astype(BF16)
    pa_ref[0] = _dot(hb, wa_ref[...])
    pb_ref[0] = _dot(hb, wb_ref[...])
    pc_ref[0] = _dot(hb, wc_ref[...])


def _in_proj(h, wa, wb, wc):
    bsz, seq, _ = h.shape
    tile = _row_tile(seq)
    return pl.pallas_call(
        functools.partial(_in_proj_kernel, tile=tile, seq=seq),
        grid=(bsz, seq // tile),
        in_specs=[_row_spec(tile, D_MODEL), _const_spec((D_MODEL, D_A_COLS)),
                  _const_spec((D_MODEL, D_B_IN)), _const_spec((D_MODEL, D_C_IN))],
        out_specs=[_row_spec(tile, D_A_COLS), _row_spec(tile, D_B_IN), _row_spec(tile, D_C_IN)],
        out_shape=[jax.ShapeDtypeStruct((bsz, seq, D_A_COLS), F32),
                   jax.ShapeDtypeStruct((bsz, seq, D_B_IN), F32),
                   jax.ShapeDtypeStruct((bsz, seq, D_C_IN), F32)],
        compiler_params=pltpu.CompilerParams(dimension_semantics=("parallel", "parallel"),
                                             vmem_limit_bytes=VMEM_LIMIT_BYTES),
        name="in_proj",
    )(h, wa, wb, wc)


def _mix_ffn_ln_kernel(h_ref, ya_ref, yb_ref, yc_ref, wmix_ref, g1_ref, b1_ref, win_ref, wout_ref,
                       g2_ref, b2_ref, o_ref):
    mix = _dot(ya_ref[0], wmix_ref[0:D_A, :])
    mix = mix + _dot(yb_ref[0], wmix_ref[D_A:D_A + D_B, :])
    mix = mix + _dot(yc_ref[0], wmix_ref[D_A + D_B:, :])
    h2 = _layer_norm(ALPHA * h_ref[0] + mix, g1_ref[...], b1_ref[...])
    o_ref[0] = _swiglu_ln(h2, win_ref, wout_ref, g2_ref, b2_ref)


def _mix_ffn_ln(h, ya, yb, yc, w_mix, g1, b1, w_in, w_out, g2, b2, layer, out_seq=None):
    bsz, seq, _ = h.shape
    out_seq = seq if out_seq is None else out_seq
    tile = _row_tile(min(seq, out_seq))
    vec = _const_spec((1, D_MODEL))
    return pl.pallas_call(
        _mix_ffn_ln_kernel,
        grid=(bsz, min(seq, out_seq) // tile),
        in_specs=[_row_spec(tile, D_MODEL), _row_spec(tile, D_A), _row_spec(tile, D_B), _row_spec(tile, D_C),
                  _layer_spec(w_mix, layer), vec, vec,
                  _layer_spec(w_in, layer), _layer_spec(w_out, layer), vec, vec],
        out_specs=_row_spec(tile, D_MODEL),
        out_shape=jax.ShapeDtypeStruct((bsz, out_seq, D_MODEL), F32),
        compiler_params=pltpu.CompilerParams(dimension_semantics=("parallel", "parallel"),
                                             vmem_limit_bytes=VMEM_LIMIT_BYTES),
        name="mix_ffn_ln",
    )(h, ya, yb, yc, w_mix, g1, b1, w_in, w_out, g2, b2)


def _bmm(a, b):
    return jnp.einsum('nij,njk->nik', a, b, preferred_element_type=F32)


def _bmm_nt(a, b):
    return jnp.einsum('nik,njk->nij', a, b, preferred_element_type=F32)


def _bmm_tn(a, b):
    return jnp.einsum('nki,nkj->nij', a, b, preferred_element_type=F32)


def _b1(mm, a, b):
    return mm(a.astype(BF16), b.astype(BF16))


def _pair_diag(y):
    left = lax.broadcasted_iota(jnp.int32, (1, 1, 2 * RWKV_HEAD), 2) < RWKV_HEAD
    return jnp.concatenate([jnp.where(left, y, 0.0), jnp.where(left, 0.0, y)], axis=1)


def _pair_series(x, eye):
    p = eye + x
    xd = _pair_diag(x).astype(BF16)
    for _ in range(5):
        x = _bmm(x.astype(BF16), xd)
        xd = _pair_diag(x).astype(BF16)
        p = p + _bmm(p.astype(BF16), xd)
    return p


def _to_chains(x3, heads, width, offset=0):
    return jnp.concatenate([x3[:, :, offset + h * width:offset + (h + 1) * width] for h in range(heads)], axis=0)


def _from_chains(xn, heads):
    bsz = xn.shape[0] // heads
    return jnp.concatenate([xn[h * bsz:(h + 1) * bsz] for h in range(heads)], axis=2)


def _block_tril(rows):
    r = lax.broadcasted_iota(jnp.int32, (rows, rows), 0)
    c = lax.broadcasted_iota(jnp.int32, (rows, rows), 1)
    return ((r // CHUNK == c // CHUNK) & (r >= c)).astype(BF16)


def _head_group_matrix(width, head, value):
    r = lax.broadcasted_iota(jnp.int32, (width, width), 0) // head
    c = lax.broadcasted_iota(jnp.int32, (width, width), 1) // head
    return jnp.where(r == c, value, 0.0).astype(BF16)


def _gdn_body(pa_ref, conv_ref, alog_ref, dt_ref, nw_ref, o_ref, cbuf, state):
    bsz = pa_ref.shape[0]
    rows = bsz * CHUNK
    causal, strict, _ = _tri_masks()

    x = pa_ref[:, :, 0:GDN_QKV]
    cbuf[:, 8:8 + CHUNK, :] = x
    y = conv_ref[3:4, :] * x
    for tap in range(GDN_CONV - 1):
        lo = 8 - (GDN_CONV - 1) + tap
        y = y + conv_ref[tap:tap + 1, :] * cbuf[:, lo:lo + CHUNK, :]
    cbuf[:, 0:8, :] = x[:, CHUNK - 8:, :]
    qkv = _silu(y)

    yield
    gates = pa_ref[:, :, GDN_QKV + D_A:].reshape(rows, GATE_PAD)
    beta_all = jax.nn.sigmoid(gates).reshape(bsz, CHUNK, GATE_PAD)
    lg_all = -jnp.exp(alog_ref[...]) * _softplus(gates + dt_ref[...])
    g_all = _mm_exact_lhs(_block_tril(rows), _split3(lg_all))
    g_t = g_all.T
    g_all = g_all.reshape(bsz, CHUNK, GATE_PAD)
    beta = jnp.concatenate([beta_all[:, :, h:h + 1] for h in range(GDN_HEADS)], axis=0)
    gc = jnp.concatenate([g_all[:, :, GDN_HEADS + h:GDN_HEADS + h + 1] for h in range(GDN_HEADS)], axis=0)
    g_row = jnp.concatenate([g_t[GDN_HEADS + h:GDN_HEADS + h + 1, b * CHUNK:(b + 1) * CHUNK][None]
                             for h in range(GDN_HEADS) for b in range(bsz)], axis=0)

    yield
    q = _to_chains(qkv, GDN_HEADS, GDN_DK)
    k = _to_chains(qkv, GDN_HEADS, GDN_DK, GDN_HEADS * GDN_DK)
    v = _to_chains(qkv, GDN_HEADS, GDN_DV, 2 * GDN_HEADS * GDN_DK)
    ones_dk = jnp.ones((GDN_DK, GDN_DK), BF16)
    lane_sum = lambda t: _bdot((t * t).reshape(-1, GDN_DK), ones_dk).reshape(t.shape)
    q = q * (lax.rsqrt(lane_sum(q) + 1e-6) * GDN_DK ** -0.5)
    k = k * lax.rsqrt(lane_sum(k) + 1e-6)
    diff = gc - g_row
    decay = jnp.where(causal, jnp.exp(jnp.where(causal, diff, 0.0)), 0.0)
    yield
    kb = k * beta
    kq = _b1(_bmm_nt, jnp.concatenate([kb, q], axis=1), k)
    a_mat = jnp.where(strict, kq[:, :CHUNK] * decay, 0.0)
    qk = kq[:, CHUNK:] * decay
    half = a_mat.shape[0] // 2
    packed = yield jnp.concatenate([-a_mat[:half], -a_mat[half:]], axis=2)
    t_inv = jnp.concatenate([packed[:, :, :CHUNK], packed[:, :, CHUNK:]], axis=0)
    e_g = jnp.exp(gc)
    sol = _b1(_bmm, t_inv, jnp.concatenate([v * beta, kb * e_g], axis=2))
    u, w = sol[:, :, :GDN_DV], sol[:, :, GDN_DV:]
    g_last = gc[:, CHUNK - 1:CHUNK, :]
    k_dec = k * jnp.exp(g_last - gc)
    s0 = state[...]
    ws = _b1(_bmm, jnp.concatenate([w, q * e_g], axis=1), s0)
    v_new = u - ws[:, :CHUNK]
    o = ws[:, CHUNK:] + _b1(_bmm, qk, v_new)
    state[...] = s0 * jnp.exp(g_last) + _b1(_bmm_tn, k_dec, v_new)
    yield
    o = o * lax.rsqrt(lane_sum(o) * (1.0 / GDN_DV) + LN_EPS) * nw_ref[...]
    z = _to_chains(pa_ref[:, :, GDN_QKV:GDN_QKV + D_A], GDN_HEADS, GDN_DV)
    o_ref[...] = _from_chains(o * _silu(z), GDN_HEADS).astype(o_ref.dtype)


def _rwkv_body(pb_ref, mu_ref, w0_ref, wup_ref, a0_ref, aup_ref, gup_ref, kk_ref, ka_ref,
               rk_ref, lg_ref, lb_ref, o_ref, sbuf, state):
    bsz = pb_ref.shape[0]
    rows = bsz * CHUNK
    head_ones = _head_group_matrix(D_B, RWKV_HEAD, 1.0)
    head_avg = _head_group_matrix(D_B, RWKV_HEAD, 1.0 / RWKV_HEAD)

    x = pb_ref[...]
    sbuf[:, 8:8 + CHUNK, :] = x
    prev = sbuf[:, 7:7 + CHUNK, :]
    sbuf[:, 0:8, :] = x[:, CHUNK - 8:, :]
    pf = (x + (prev - x) * mu_ref[...]).reshape(rows, D_B_IN)
    r_all, k_all, v_all = pf[:, 0:D_B], pf[:, D_B:2 * D_B], pf[:, 2 * D_B:3 * D_B]
    c0 = 3 * D_B
    xw = pf[:, c0:c0 + RWKV_DECAY_RANK]
    xa = pf[:, c0 + RWKV_DECAY_RANK:c0 + RWKV_DECAY_RANK + RWKV_A_RANK]
    xg = pf[:, c0 + RWKV_DECAY_RANK + RWKV_A_RANK:]
    log_w = -_softplus(-(w0_ref[...] + _mm3(jnp.tanh(xw), wup_ref[...]))) - 0.5
    wlog = -jnp.exp(log_w)
    a_all = jax.nn.sigmoid(a0_ref[...] + _mm3(xa, aup_ref[...]))
    gate_all = _bdot(jax.nn.sigmoid(xg), gup_ref[...])
    yield
    kk = k_all * kk_ref[...]
    kk = kk * lax.rsqrt(_bdot(kk * kk, head_ones) + 1e-6)
    k2 = k_all * (1.0 + (a_all - 1.0) * ka_ref[...])
    bb = kk * a_all
    gc = _mm_exact_lhs(_block_tril(rows), _split2(wlog))
    e_neg = jnp.exp(-gc)
    r_t = r_all * jnp.exp(gc)
    a_t = -kk * jnp.exp(gc - wlog)
    k_t = k2 * e_neg
    b_t = bb * e_neg
    gc3 = gc.reshape(bsz, CHUNK, D_B)
    g_last = gc3[:, CHUNK - 1:CHUNK, :]
    e_tail = jnp.exp(g_last - gc3).reshape(rows, D_B)
    bonus = _bdot(r_all * k2 * rk_ref[...], head_ones) * v_all

    yield
    pair_w = 2 * RWKV_HEAD
    n_pairs = RWKV_HEADS // 2
    pairs = lambda t: _to_chains(t.reshape(bsz, CHUNK, D_B), n_pairs, pair_w)
    a_n, r_n, b_n, k_n, v_n = pairs(a_t), pairs(r_t), pairs(b_t), pairs(k_t), pairs(v_all)
    bdec_n, kdec_n = pairs(bb * e_tail), pairs(k2 * e_tail)
    elast_n = _to_chains(jnp.exp(g_last), n_pairs, pair_w)
    prow = lax.broadcasted_iota(jnp.int32, (CHUNK, pair_w), 0)
    pcol = lax.broadcasted_iota(jnp.int32, (CHUNK, pair_w), 1) % RWKV_HEAD
    causal2, strict2, eye2 = prow >= pcol, prow > pcol, (prow == pcol).astype(F32)
    srow = lax.broadcasted_iota(jnp.int32, (pair_w, pair_w), 0) // RWKV_HEAD
    scol = lax.broadcasted_iota(jnp.int32, (pair_w, pair_w), 1) // RWKV_HEAD
    same_head = srow == scol

    ar = jnp.concatenate([a_n, r_n], axis=1).astype(BF16)
    xb = _bmm_nt(ar, _pair_diag(b_n).astype(BF16))
    xk = _bmm_nt(ar, _pair_diag(k_n).astype(BF16))
    a_ab = jnp.where(strict2, xb[:, :CHUNK], 0.0)
    a_rb = jnp.where(causal2, xb[:, CHUNK:], 0.0)
    a_k = jnp.concatenate([jnp.where(strict2, xk[:, :CHUNK], 0.0), jnp.where(causal2, xk[:, CHUNK:], 0.0)], axis=1)
    m_inv = (yield a_ab).astype(BF16)
    akv = _bmm(a_k.astype(BF16), _pair_diag(v_n).astype(BF16))
    w_mat = _bmm(m_inv, _pair_diag(a_n).astype(BF16))
    u2 = _bmm(m_inv, _pair_diag(akv[:, :CHUNK]).astype(BF16))
    s0 = state[...]
    wr = _b1(_bmm_nt, jnp.concatenate([w_mat, r_n], axis=1), s0)
    u = wr[:, :CHUNK] + u2
    y_n = wr[:, CHUNK:] + _b1(_bmm, a_rb, _pair_diag(u)) + akv[:, CHUNK:]
    upd = _b1(_bmm_tn, jnp.concatenate([u, v_n], axis=1), jnp.concatenate([bdec_n, kdec_n], axis=1))
    state[...] = s0 * elast_n + jnp.where(same_head, upd, 0.0)

    yield
    y = _from_chains(y_n, n_pairs).reshape(rows, D_B)
    mu = _bdot(y, head_avg)
    d = y - mu
    var = _bdot(d * d, head_avg)
    yn = d * lax.rsqrt(var + RWKV_LNX_EPS) * lg_ref[...] + lb_ref[...]
    o_ref[...] = ((yn + bonus) * gate_all).reshape(bsz, CHUNK, D_B).astype(o_ref.dtype)


def _ret_log_gamma(h):
    return math.log(1.0 - 2.0 ** (-5.0 - h))


def _ret_body(pc_ref, cos_ref, sin_ref, ng_ref, nb_ref, o_ref, state):
    bsz = pc_ref.shape[0]
    rows = bsz * CHUNK
    qk_w = RET_HEADS * RET_DK
    lane = lax.broadcasted_iota(jnp.int32, (bsz, CHUNK, qk_w), 2)
    first_half = (lane % RET_DK) < (RET_DK // 2)

    def rope(t):
        partner = jnp.where(first_half, pltpu.roll(t, qk_w - RET_DK // 2, 2), pltpu.roll(t, RET_DK // 2, 2))
        return t * cos_ref[...] + partner * sin_ref[...]

    row = lax.broadcasted_iota(jnp.int32, (CHUNK, CHUNK), 0)
    col = lax.broadcasted_iota(jnp.int32, (CHUNK, CHUNK), 1)
    delta = (row - col).astype(F32)
    idx = lax.broadcasted_iota(jnp.int32, (CHUNK, 1), 0).astype(F32)
    head_avg = _head_group_matrix(D_C, RET_DV, 1.0 / RET_DV)
    rep = lambda t: jnp.broadcast_to(t[None], (bsz,) + t.shape)
    d_intra = jnp.concatenate(
        [rep(jnp.where(delta >= 0, jnp.exp(_ret_log_gamma(h) * jnp.maximum(delta, 0.0)), 0.0))
         for h in range(RET_HEADS)], axis=0)
    q_scale = jnp.concatenate([rep(jnp.exp(_ret_log_gamma(h) * (idx + 1.0))) for h in range(RET_HEADS)], axis=0)
    k_scale = jnp.concatenate([rep(jnp.exp(_ret_log_gamma(h) * (CHUNK - 1.0 - idx))) for h in range(RET_HEADS)],
                              axis=0)
    s_scale = jnp.concatenate([jnp.full((bsz, 1, 1), math.exp(_ret_log_gamma(h) * CHUNK), F32)
                               for h in range(RET_HEADS)], axis=0)

    yield
    q = _to_chains(rope(pc_ref[:, :, 0:qk_w]), RET_HEADS, RET_DK)
    k = _to_chains(rope(pc_ref[:, :, qk_w:2 * qk_w]) * RET_DK ** -0.5, RET_HEADS, RET_DK)
    v = _to_chains(pc_ref[:, :, 2 * qk_w:2 * qk_w + D_C], RET_HEADS, RET_DV)
    intra = _b1(_bmm, _b1(_bmm_nt, q, k) * d_intra, v)
    s0 = state[...]
    y_n = intra + _b1(_bmm, q * q_scale, s0)
    state[...] = s0 * s_scale + _b1(_bmm_tn, k * k_scale, v)
    yield
    y = _from_chains(y_n, RET_HEADS).reshape(rows, D_C)
    mu = _bdot(y, head_avg)
    d = y - mu
    var = _bdot(d * d, head_avg)
    yn = d * lax.rsqrt(var + LN_EPS) * ng_ref[...] + nb_ref[...]
    gate = pc_ref[:, :, 2 * qk_w + D_C:].reshape(rows, D_C)
    o_ref[...] = (yn * _silu(gate)).reshape(bsz, CHUNK, D_C).astype(o_ref.dtype)


N_GDN_PARAMS, N_RWKV_PARAMS, N_RET_PARAMS = 4, 11, 4


def _mixers_kernel(*refs):
    pa_ref, pb_ref, pc_ref = refs[:3]
    params = refs[3:3 + N_GDN_PARAMS + N_RWKV_PARAMS + N_RET_PARAMS]
    gdn_p, rwkv_p = params[:N_GDN_PARAMS], params[N_GDN_PARAMS:N_GDN_PARAMS + N_RWKV_PARAMS]
    ret_p = params[N_GDN_PARAMS + N_RWKV_PARAMS:]
    oa_ref, ob_ref, oc_ref, cbuf, gdn_state, sbuf, rwkv_state, ret_state = refs[3 + len(params):]

    @pl.when(pl.program_id(0) == 0)
    def _():
        cbuf[:, 0:8, :] = jnp.zeros((cbuf.shape[0], 8, GDN_QKV), F32)
        sbuf[:, 0:8, :] = jnp.zeros((sbuf.shape[0], 8, D_B_IN), F32)
        gdn_state[...] = jnp.zeros(gdn_state.shape, F32)
        rwkv_state[...] = jnp.zeros(rwkv_state.shape, F32)
        ret_state[...] = jnp.zeros(ret_state.shape, F32)

    bodies = [_gdn_body(pa_ref, *gdn_p, oa_ref, cbuf, gdn_state),
              _rwkv_body(pb_ref, *rwkv_p, ob_ref, sbuf, rwkv_state),
              _ret_body(pc_ref, *ret_p, oc_ref, ret_state)]
    prow = lax.broadcasted_iota(jnp.int32, (CHUNK, 2 * CHUNK), 0)
    pcol = lax.broadcasted_iota(jnp.int32, (CHUNK, 2 * CHUNK), 1) % CHUNK
    eye2 = (prow == pcol).astype(F32)
    inbox = {body: None for body in bodies}
    waiting = {}
    while bodies:
        for body in list(bodies):
            if body in waiting:
                continue
            try:
                request = body.send(inbox[body])
            except StopIteration:
                bodies.remove(body)
                continue
            inbox[body] = None
            if request is not None:
                waiting[body] = request
        if len(waiting) == 2 or (waiting and all(body in waiting for body in bodies)):
            owners = list(waiting)
            inv = _pair_series(jnp.concatenate([waiting[o] for o in owners], axis=0), eye2)
            start = 0
            for o in owners:
                n = waiting[o].shape[0]
                inbox[o] = inv[start:start + n]
                start += n
            waiting.clear()


def _mixers(pa, pb, pc, gdn_params, rwkv_params, ret_params):
    bsz, seq, _ = pa.shape
    n_chunks = seq // CHUNK
    chunked = lambda width: pl.BlockSpec((bsz, CHUNK, width), lambda c: (0, (c + n_chunks - 1) % n_chunks, 0))
    rope_spec = pl.BlockSpec((CHUNK, RET_HEADS * RET_DK), lambda c: (c, 0))
    params = tuple(gdn_params) + tuple(rwkv_params) + tuple(ret_params)
    param_specs = [_const_spec(p.shape) for p in gdn_params + rwkv_params] + \
                  [rope_spec, rope_spec] + [_const_spec(p.shape) for p in ret_params[2:]]
    return pl.pallas_call(
        _mixers_kernel,
        grid=(n_chunks,),
        in_specs=[chunked(D_A_COLS), chunked(D_B_IN), chunked(D_C_IN)] + param_specs,
        out_specs=[chunked(D_A), chunked(D_B), chunked(D_C)],
        out_shape=[jax.ShapeDtypeStruct((bsz, seq, D_A), BF16),
                   jax.ShapeDtypeStruct((bsz, seq, D_B), BF16),
                   jax.ShapeDtypeStruct((bsz, seq, D_C), BF16)],
        scratch_shapes=[pltpu.VMEM((bsz, 8 + CHUNK, GDN_QKV), F32),
                        pltpu.VMEM((GDN_HEADS * bsz, GDN_DK, GDN_DV), F32),
                        pltpu.VMEM((bsz, 8 + CHUNK, D_B_IN), F32),
                        pltpu.VMEM((RWKV_HEADS // 2 * bsz, 2 * RWKV_HEAD, 2 * RWKV_HEAD), F32),
                        pltpu.VMEM((RET_HEADS * bsz, RET_DK, RET_DV), F32)],
        compiler_params=pltpu.CompilerParams(dimension_semantics=("arbitrary",),
                                             vmem_limit_bytes=VMEM_LIMIT_BYTES),
        name="mixers",
    )(pa, pb, pc, *params)


def _rope_tables(seq):
    half = RET_DK // 2
    inv_freq = 1.0 / (ROPE_BASE ** jnp.linspace(0.0, 1.0, half, dtype=F32))
    ang = jnp.arange(seq, dtype=F32)[:, None] * inv_freq
    cos, sin = jnp.cos(ang), jnp.sin(ang)
    cos_t = jnp.tile(jnp.concatenate([cos, cos], -1), (1, RET_HEADS))
    sin_t = jnp.tile(jnp.concatenate([-sin, sin], -1), (1, RET_HEADS))
    return cos_t, sin_t


def _row(v):
    return v.reshape(1, -1)


def _pad_gate_vec(v):
    return jnp.zeros((1, GATE_PAD), F32).at[0, GDN_HEADS:2 * GDN_HEADS].set(v)


def _mixer_layer(h3, w_in, gdn, rwkv, ret, rope):
    wa = jnp.concatenate([w_in[:, :D_A_IN], jnp.zeros((D_MODEL, D_A_COLS - D_A_IN), w_in.dtype)], axis=1)
    wb = w_in[:, D_A_IN:D_A_IN + D_B_IN]
    wc = w_in[:, D_A_IN + D_B_IN:]
    pa, pb, pc = _in_proj(h3, wa.astype(BF16), wb.astype(BF16), wc.astype(BF16))
    conv_w, a_log, dt_bias, norm_w = gdn
    return _mixers(
        pa, pb, pc,
        [conv_w, _pad_gate_vec(a_log), _pad_gate_vec(dt_bias), _row(norm_w)],
        [_row(p) if p.ndim == 1 else p for p in rwkv],
        [rope[0], rope[1], _row(ret[0]), _row(ret[1])])


def kernel(x, meta_tokens, ln_g, ln_b, w_ff1_in, w_ff1_out, w_ff2_in, w_ff2_out, w_in, w_out,
           gdn_conv_w, gdn_a_log, gdn_dt_bias, gdn_norm_w, rwkv_mu, rwkv_w0, rwkv_w_up, rwkv_a0,
           rwkv_a_up, rwkv_g_up, rwkv_k_k, rwkv_k_a, rwkv_r_k, rwkv_lnx_g, rwkv_lnx_b,
           ret_norm_g, ret_norm_b):
    bsz, n_real, d = x.shape
    seq = n_real + CHUNK
    depth = ln_g.shape[0]
    rope = _rope_tables(seq)
    meta_chunk = jnp.concatenate([jnp.zeros((META_PAD, d), x.dtype), meta_tokens.astype(x.dtype)], axis=0)
    ff1_in, ff1_out, ff2_in, ff2_out, w_mix = (w.astype(BF16) for w in (w_ff1_in, w_ff1_out, w_ff2_in, w_ff2_out, w_out))
    h = None
    for l in range(depth):
        ffn1 = (ff1_in, ff1_out, _row(ln_g[l, 0]), _row(ln_b[l, 0]), l)
        if l == 0:
            h = _first_ffn_ln(x, meta_chunk, *ffn1)
        else:
            h = _ffn_ln(h, *ffn1)
        ya, yb, yc = _mixer_layer(
            h, w_in[l],
            (gdn_conv_w[l], gdn_a_log[l], gdn_dt_bias[l], gdn_norm_w[l]),
            (rwkv_mu[l], rwkv_w0[l], rwkv_w_up[l], rwkv_a0[l], rwkv_a_up[l], rwkv_g_up[l],
             rwkv_k_k[l], rwkv_k_a[l], rwkv_r_k[l].reshape(-1), rwkv_lnx_g[l], rwkv_lnx_b[l]),
            (ret_norm_g[l], ret_norm_b[l]), rope)
        h = _mix_ffn_ln(h, ya, yb, yc, w_mix, _row(ln_g[l, 1]), _row(ln_b[l, 1]),
                        ff2_in, ff2_out, _row(ln_g[l, 2]), _row(ln_b[l, 2]), l,
                        out_seq=n_real if l == depth - 1 else None)
    return h
```

```python
import functools
import math

import jax
import jax.numpy as jnp
from jax import lax
from jax.experimental import pallas as pl
from jax.experimental.pallas import tpu as pltpu

F32 = jnp.float32
BF16 = jnp.bfloat16

D_MODEL = 1024
DEPTH = 2
CHUNK = 64
N_META = 16
META_PAD = CHUNK - N_META
D_FF = 2816
LN_EPS = 1e-5
ALPHA = (2.0 * DEPTH) ** 0.25

LANES = 128
SUBLANES = 8
MXU_TILE = 256
VMEM_LIMIT_BYTES = 56 * 1024 * 1024
HALO = SUBLANES

GDN_HEADS, GDN_DK, GDN_DV, GDN_CONV = 4, 128, 128, 4
GDN_QKV = 2 * GDN_HEADS * GDN_DK + GDN_HEADS * GDN_DV
D_A = GDN_HEADS * GDN_DV
GATE_PAD = LANES
D_A_COLS = GDN_QKV + D_A + GATE_PAD
RWKV_HEADS, RWKV_HEAD = 4, 64
RWKV_DECAY_RANK, RWKV_A_RANK, RWKV_GATE_RANK = 32, 32, 64
RWKV_LNX_EPS = 64e-5
D_B = RWKV_HEADS * RWKV_HEAD
D_B_IN = 3 * D_B + RWKV_DECAY_RANK + RWKV_A_RANK + RWKV_GATE_RANK
RET_HEADS, RET_DK, RET_DV = 4, 32, 64
ROPE_BASE = 10000.0
D_C = RET_HEADS * RET_DV
D_C_IN = 2 * RET_HEADS * RET_DK + 2 * D_C
D_A_IN = GDN_QKV + D_A + 2 * GDN_HEADS

ROW_TILES = (1032, 1024, 768, 512, 256, 128, CHUNK)
FF_CHUNKS = ((0, 4 * MXU_TILE), (4 * MXU_TILE, 4 * MXU_TILE), (8 * MXU_TILE, 3 * MXU_TILE))
assert sum(w for _, w in FF_CHUNKS) == D_FF and all(s % SUBLANES == 0 for s in ROW_TILES)


def _dot(a, b):
    return jnp.dot(a, b, preferred_element_type=F32)


def _bdot(a, b):
    return _dot(a.astype(BF16), b.astype(BF16))


def _layer_norm(y, g, b):
    mu = jnp.mean(y, axis=-1, keepdims=True)
    d = y - mu
    var = jnp.mean(d * d, axis=-1, keepdims=True)
    return d * lax.rsqrt(var + LN_EPS) * g + b


def _silu(x):
    return x * jax.nn.sigmoid(x)


def _softplus(x):
    return jnp.maximum(x, 0.0) + jnp.log(1.0 + jnp.exp(-jnp.abs(x)))


def _tri_masks():
    row = lax.broadcasted_iota(jnp.int32, (CHUNK, CHUNK), 0)
    col = lax.broadcasted_iota(jnp.int32, (CHUNK, CHUNK), 1)
    return row >= col, row > col, row == col


def _split2(x):
    hi = x.astype(BF16)
    lo = (x - hi.astype(F32)).astype(BF16)
    return hi, lo


def _split3(x):
    hi = x.astype(BF16)
    r1 = x - hi.astype(F32)
    mid = r1.astype(BF16)
    lo = (r1 - mid.astype(F32)).astype(BF16)
    return hi, mid, lo


def _mm3(a, b):
    ah, al = _split2(a)
    bh, bl = _split2(b)
    return _dot(ah, bh) + (_dot(ah, bl) + _dot(al, bh))


def _mm_exact_lhs(a_bf16, pieces):
    out = _dot(a_bf16, pieces[0])
    for p in pieces[1:]:
        out = out + _dot(a_bf16, p)
    return out


def _row_tile(n):
    for cand in ROW_TILES:
        if n % cand == 0:
            return cand
    return n


def _const_spec(shape):
    return pl.BlockSpec(shape, lambda *_: (0,) * len(shape), pipeline_mode=pl.Buffered(1))


def _layer_spec(w_stack, layer):
    _, rows, cols = w_stack.shape
    return pl.BlockSpec((None, rows, cols), lambda *_: (layer, 0, 0), pipeline_mode=pl.Buffered(1))


def _row_spec(tile, width):
    return pl.BlockSpec((1, tile, width), lambda b, j: (b, j, 0))


def _swiglu_ln(x, win_ref, wout_ref, g_ref, b_ref):
    xb = x.astype(BF16)
    acc = jnp.zeros(x.shape, F32)
    for start, width in FF_CHUNKS:
        gate = _dot(xb, win_ref[:, start:start + width])
        up = _dot(xb, win_ref[:, D_FF + start:D_FF + start + width])
        act = (_silu(gate) * up).astype(BF16)
        acc = acc + _dot(act, wout_ref[start:start + width, :])
    return _layer_norm(ALPHA * x + 0.5 * acc, g_ref[...], b_ref[...])


def _ffn_ln_kernel(x_ref, win_ref, wout_ref, g_ref, b_ref, o_ref):
    o_ref[0] = _swiglu_ln(x_ref[0], win_ref, wout_ref, g_ref, b_ref)


def _ffn_ln(h3, w_in, w_out, g, b, layer):
    bsz, seq, _ = h3.shape
    tile = _row_tile(seq)
    return pl.pallas_call(
        _ffn_ln_kernel,
        grid=(bsz, seq // tile),
        in_specs=[_row_spec(tile, D_MODEL),
                  _layer_spec(w_in, layer),
                  _layer_spec(w_out, layer),
                  _const_spec((1, D_MODEL)),
                  _const_spec((1, D_MODEL))],
        out_specs=_row_spec(tile, D_MODEL),
        out_shape=jax.ShapeDtypeStruct((bsz, seq, D_MODEL), F32),
        compiler_params=pltpu.CompilerParams(dimension_semantics=("parallel", "parallel"),
                                             vmem_limit_bytes=VMEM_LIMIT_BYTES),
        name="ffn_ln",
    )(h3, w_in, w_out, g, b)


def _first_ffn_ln_kernel(x_ref, meta_ref, win_ref, wout_ref, g_ref, b_ref, o_ref, *, n_main):
    j = pl.program_id(1)

    @pl.when(j < n_main)
    def _():
        o_ref[0] = _swiglu_ln(x_ref[0], win_ref, wout_ref, g_ref, b_ref)

    @pl.when(j == n_main)
    def _():
        o_ref[0, 0:CHUNK, :] = _swiglu_ln(meta_ref[...], win_ref, wout_ref, g_ref, b_ref)


def _first_ffn_ln(x, meta_chunk, w_in, w_out, g, b, layer):
    bsz, n_real, _ = x.shape
    tile = _row_tile(n_real)
    n_main = n_real // tile
    return pl.pallas_call(
        functools.partial(_first_ffn_ln_kernel, n_main=n_main),
        grid=(bsz, n_main + 1),
        in_specs=[pl.BlockSpec((1, tile, D_MODEL), lambda bi, j: (bi, jnp.minimum(j, n_main - 1), 0)),
                  _const_spec((CHUNK, D_MODEL)),
                  _layer_spec(w_in, layer),
                  _layer_spec(w_out, layer),
                  _const_spec((1, D_MODEL)),
                  _const_spec((1, D_MODEL))],
        out_specs=_row_spec(tile, D_MODEL),
        out_shape=jax.ShapeDtypeStruct((bsz, n_real + CHUNK, D_MODEL), F32),
        compiler_params=pltpu.CompilerParams(dimension_semantics=("parallel", "arbitrary"),
                                             vmem_limit_bytes=VMEM_LIMIT_BYTES),
        name="first_ffn_ln",
    )(x, meta_chunk, w_in, w_out, g, b)


def _in_proj_kernel(h_ref, wa_ref, wb_ref, wc_ref, pa_ref, pb_ref, pc_ref, *, tile, seq):
    pos = pl.program_id(1) * tile + lax.broadcasted_iota(jnp.int32, (tile, 1), 0)
    inert = (pos >= seq - CHUNK) & (pos < seq - N_META)
    hb = jnp.where(inert, 0.0, h_ref[0]).astype(BF16)
    pa_ref[0] = _dot(hb, wa_ref[...])
    pb_ref[0] = _dot(hb, wb_ref[...])
    pc_ref[0] = _dot(hb, wc_ref[...])


def _in_proj(h, wa, wb, wc):
    bsz, seq, _ = h.shape
    tile = _row_tile(seq)
    return pl.pallas_call(
        functools.partial(_in_proj_kernel, tile=tile, seq=seq),
        grid=(bsz, seq // tile),
        in_specs=[_row_spec(tile, D_MODEL), _const_spec((D_MODEL, D_A_COLS)),
                  _const_spec((D_MODEL, D_B_IN)), _const_spec((D_MODEL, D_C_IN))],
        out_specs=[_row_spec(tile, D_A_COLS), _row_spec(tile, D_B_IN), _row_spec(tile, D_C_IN)],
        out_shape=[jax.ShapeDtypeStruct((bsz, seq, D_A_COLS), F32),
                   jax.ShapeDtypeStruct((bsz, seq, D_B_IN), F32),
                   jax.ShapeDtypeStruct((bsz, seq, D_C_IN), F32)],
        compiler_params=pltpu.CompilerParams(dimension_semantics=("parallel", "parallel"),
                                             vmem_limit_bytes=VMEM_LIMIT_BYTES),
        name="in_proj",
    )(h, wa, wb, wc)


def _mix_ffn_ln_kernel(h_ref, y_ref, wmix_ref, g1_ref, b1_ref, win_ref, wout_ref, g2_ref, b2_ref, o_ref):
    mix = _dot(y_ref[0], wmix_ref[...])
    h2 = _layer_norm(ALPHA * h_ref[0] + mix, g1_ref[...], b1_ref[...])
    o_ref[0] = _swiglu_ln(h2, win_ref, wout_ref, g2_ref, b2_ref)


def _mix_ffn_ln(h, y, w_mix, g1, b1, w_in, w_out, g2, b2, layer, out_seq=None):
    bsz, seq, _ = h.shape
    out_seq = seq if out_seq is None else out_seq
    tile = _row_tile(min(seq, out_seq))
    vec = _const_spec((1, D_MODEL))
    return pl.pallas_call(
        _mix_ffn_ln_kernel,
        grid=(bsz, min(seq, out_seq) // tile),
        in_specs=[_row_spec(tile, D_MODEL), _row_spec(tile, D_A + D_B + D_C),
                  _layer_spec(w_mix, layer), vec, vec,
                  _layer_spec(w_in, layer), _layer_spec(w_out, layer), vec, vec],
        out_specs=_row_spec(tile, D_MODEL),
        out_shape=jax.ShapeDtypeStruct((bsz, out_seq, D_MODEL), F32),
        compiler_params=pltpu.CompilerParams(dimension_semantics=("parallel", "parallel"),
                                             vmem_limit_bytes=VMEM_LIMIT_BYTES),
        name="mix_ffn_ln",
    )(h, y, w_mix, g1, b1, w_in, w_out, g2, b2)


def _bmm(a, b):
    return jnp.einsum('nij,njk->nik', a, b, preferred_element_type=F32)


def _bmm_nt(a, b):
    return jnp.einsum('nik,njk->nij', a, b, preferred_element_type=F32)


def _bmm_tn(a, b):
    return jnp.einsum('nki,nkj->nij', a, b, preferred_element_type=F32)


def _b1(mm, a, b):
    return mm(a.astype(BF16), b.astype(BF16))


def _pair_diag(y):
    left = lax.broadcasted_iota(jnp.int32, (1, 1, 2 * RWKV_HEAD), 2) < RWKV_HEAD
    return jnp.concatenate([jnp.where(left, y, 0.0), jnp.where(left, 0.0, y)], axis=1)


def _pair_series(x, eye):
    p = eye + x
    xd = _pair_diag(x).astype(BF16)
    for _ in range(5):
        x = _bmm(x.astype(BF16), xd)
        xd = _pair_diag(x).astype(BF16)
        p = p + _bmm(p.astype(BF16), xd)
    return p


def _to_chains(x3, heads, width, offset=0):
    return jnp.concatenate([x3[:, :, offset + h * width:offset + (h + 1) * width] for h in range(heads)], axis=0)


def _from_chains(xn, heads):
    bsz = xn.shape[0] // heads
    return jnp.concatenate([xn[h * bsz:(h + 1) * bsz] for h in range(heads)], axis=2)


def _block_tril(rows):
    r = lax.broadcasted_iota(jnp.int32, (rows, rows), 0)
    c = lax.broadcasted_iota(jnp.int32, (rows, rows), 1)
    return ((r // CHUNK == c // CHUNK) & (r >= c)).astype(BF16)


def _head_group_matrix(width, head, value):
    r = lax.broadcasted_iota(jnp.int32, (width, width), 0) // head
    c = lax.broadcasted_iota(jnp.int32, (width, width), 1) // head
    return jnp.where(r == c, value, 0.0).astype(BF16)


def _gdn_body(pa_ref, conv_ref, alog_ref, dt_ref, nw_ref, o_ref, cbuf, state):
    bsz = pa_ref.shape[0]
    rows = bsz * CHUNK
    causal, strict, _ = _tri_masks()

    x = pa_ref[:, :, 0:GDN_QKV]
    cbuf[:, HALO:HALO + CHUNK, :] = x
    y = conv_ref[GDN_CONV - 1:GDN_CONV, :] * x
    for tap in range(GDN_CONV - 1):
        lo = HALO - (GDN_CONV - 1) + tap
        y = y + conv_ref[tap:tap + 1, :] * cbuf[:, lo:lo + CHUNK, :]
    cbuf[:, 0:HALO, :] = x[:, CHUNK - HALO:, :]
    qkv = _silu(y)

    yield
    gates = pa_ref[:, :, GDN_QKV + D_A:].reshape(rows, GATE_PAD)
    beta_all = jax.nn.sigmoid(gates).reshape(bsz, CHUNK, GATE_PAD)
    lg_all = -jnp.exp(alog_ref[...]) * _softplus(gates + dt_ref[...])
    g_all = _mm_exact_lhs(_block_tril(rows), _split3(lg_all))
    g_t = g_all.T
    g_all = g_all.reshape(bsz, CHUNK, GATE_PAD)
    beta = jnp.concatenate([beta_all[:, :, h:h + 1] for h in range(GDN_HEADS)], axis=0)
    gc = jnp.concatenate([g_all[:, :, GDN_HEADS + h:GDN_HEADS + h + 1] for h in range(GDN_HEADS)], axis=0)
    g_row = jnp.concatenate([g_t[GDN_HEADS + h:GDN_HEADS + h + 1, b * CHUNK:(b + 1) * CHUNK][None]
                             for h in range(GDN_HEADS) for b in range(bsz)], axis=0)

    yield
    q = _to_chains(qkv, GDN_HEADS, GDN_DK)
    k = _to_chains(qkv, GDN_HEADS, GDN_DK, GDN_HEADS * GDN_DK)
    v = _to_chains(qkv, GDN_HEADS, GDN_DV, 2 * GDN_HEADS * GDN_DK)
    ones_dk = jnp.ones((GDN_DK, GDN_DK), BF16)
    lane_sum = lambda t: _bdot((t * t).reshape(-1, GDN_DK), ones_dk).reshape(t.shape)
    q = q * (lax.rsqrt(lane_sum(q) + 1e-6) * GDN_DK ** -0.5)
    k = k * lax.rsqrt(lane_sum(k) + 1e-6)
    diff = gc - g_row
    decay = jnp.where(causal, jnp.exp(jnp.where(causal, diff, 0.0)), 0.0)
    yield
    kb = k * beta
    kq = _b1(_bmm_nt, jnp.concatenate([kb, q], axis=1), k)
    a_mat = jnp.where(strict, kq[:, :CHUNK] * decay, 0.0)
    qk = kq[:, CHUNK:] * decay
    half = a_mat.shape[0] // 2
    packed = yield jnp.concatenate([-a_mat[:half], -a_mat[half:]], axis=2)
    t_inv = jnp.concatenate([packed[:, :, :CHUNK], packed[:, :, CHUNK:]], axis=0)
    e_g = jnp.exp(gc)
    sol = _b1(_bmm, t_inv, jnp.concatenate([v * beta, kb * e_g], axis=2))
    u, w = sol[:, :, :GDN_DV], sol[:, :, GDN_DV:]
    g_last = gc[:, CHUNK - 1:CHUNK, :]
    k_dec = k * jnp.exp(g_last - gc)
    s0 = state[...]
    ws = _b1(_bmm, jnp.concatenate([w, q * e_g], axis=1), s0)
    v_new = u - ws[:, :CHUNK]
    o = ws[:, CHUNK:] + _b1(_bmm, qk, v_new)
    state[...] = s0 * jnp.exp(g_last) + _b1(_bmm_tn, k_dec, v_new)
    yield
    o = o * lax.rsqrt(lane_sum(o) * (1.0 / GDN_DV) + LN_EPS) * nw_ref[...]
    z = _to_chains(pa_ref[:, :, GDN_QKV:GDN_QKV + D_A], GDN_HEADS, GDN_DV)
    o_ref[...] = _from_chains(o * _silu(z), GDN_HEADS).astype(o_ref.dtype)


def _rwkv_body(pb_ref, mu_ref, w0_ref, wup_ref, a0_ref, aup_ref, gup_ref, kk_ref, ka_ref,
               rk_ref, lg_ref, lb_ref, o_ref, sbuf, state):
    bsz = pb_ref.shape[0]
    rows = bsz * CHUNK
    head_ones = _head_group_matrix(D_B, RWKV_HEAD, 1.0)
    head_avg = _head_group_matrix(D_B, RWKV_HEAD, 1.0 / RWKV_HEAD)

    x = pb_ref[...]
    sbuf[:, HALO:HALO + CHUNK, :] = x
    prev = sbuf[:, HALO - 1:HALO - 1 + CHUNK, :]
    sbuf[:, 0:HALO, :] = x[:, CHUNK - HALO:, :]
    pf = (x + (prev - x) * mu_ref[...]).reshape(rows, D_B_IN)
    r_all, k_all, v_all = pf[:, 0:D_B], pf[:, D_B:2 * D_B], pf[:, 2 * D_B:3 * D_B]
    c0 = 3 * D_B
    xw = pf[:, c0:c0 + RWKV_DECAY_RANK]
    xa = pf[:, c0 + RWKV_DECAY_RANK:c0 + RWKV_DECAY_RANK + RWKV_A_RANK]
    xg = pf[:, c0 + RWKV_DECAY_RANK + RWKV_A_RANK:]
    log_w = -_softplus(-(w0_ref[...] + _mm3(jnp.tanh(xw), wup_ref[...]))) - 0.5
    wlog = -jnp.exp(log_w)
    a_all = jax.nn.sigmoid(a0_ref[...] + _mm3(xa, aup_ref[...]))
    gate_all = _bdot(jax.nn.sigmoid(xg), gup_ref[...])
    yield
    kk = k_all * kk_ref[...]
    kk = kk * lax.rsqrt(_bdot(kk * kk, head_ones) + 1e-6)
    k2 = k_all * (1.0 + (a_all - 1.0) * ka_ref[...])
    bb = kk * a_all
    gc = _mm_exact_lhs(_block_tril(rows), _split2(wlog))
    e_neg = jnp.exp(-gc)
    r_t = r_all * jnp.exp(gc)
    a_t = -kk * jnp.exp(gc - wlog)
    k_t = k2 * e_neg
    b_t = bb * e_neg
    gc3 = gc.reshape(bsz, CHUNK, D_B)
    g_last = gc3[:, CHUNK - 1:CHUNK, :]
    e_tail = jnp.exp(g_last - gc3).reshape(rows, D_B)
    bonus = _bdot(r_all * k2 * rk_ref[...], head_ones) * v_all

    yield
    pair_w = 2 * RWKV_HEAD
    n_pairs = RWKV_HEADS // 2
    pairs = lambda t: _to_chains(t.reshape(bsz, CHUNK, D_B), n_pairs, pair_w)
    a_n, r_n, b_n, k_n, v_n = pairs(a_t), pairs(r_t), pairs(b_t), pairs(k_t), pairs(v_all)
    bdec_n, kdec_n = pairs(bb * e_tail), pairs(k2 * e_tail)
    elast_n = _to_chains(jnp.exp(g_last), n_pairs, pair_w)
    prow = lax.broadcasted_iota(jnp.int32, (CHUNK, pair_w), 0)
    pcol = lax.broadcasted_iota(jnp.int32, (CHUNK, pair_w), 1) % RWKV_HEAD
    causal2, strict2, eye2 = prow >= pcol, prow > pcol, (prow == pcol).astype(F32)
    srow = lax.broadcasted_iota(jnp.int32, (pair_w, pair_w), 0) // RWKV_HEAD
    scol = lax.broadcasted_iota(jnp.int32, (pair_w, pair_w), 1) // RWKV_HEAD
    same_head = srow == scol

    ar = jnp.concatenate([a_n, r_n], axis=1).astype(BF16)
    xb = _bmm_nt(ar, _pair_diag(b_n).astype(BF16))
    xk = _bmm_nt(ar, _pair_diag(k_n).astype(BF16))
    a_ab = jnp.where(strict2, xb[:, :CHUNK], 0.0)
    a_rb = jnp.where(causal2, xb[:, CHUNK:], 0.0)
    a_k = jnp.concatenate([jnp.where(strict2, xk[:, :CHUNK], 0.0), jnp.where(causal2, xk[:, CHUNK:], 0.0)], axis=1)
    m_inv = (yield a_ab).astype(BF16)
    akv = _bmm(a_k.astype(BF16), _pair_diag(v_n).astype(BF16))
    w_mat = _bmm(m_inv, _pair_diag(a_n).astype(BF16))
    u2 = _bmm(m_inv, _pair_diag(akv[:, :CHUNK]).astype(BF16))
    s0 = state[...]
    wr = _b1(_bmm_nt, jnp.concatenate([w_mat, r_n], axis=1), s0)
    u = wr[:, :CHUNK] + u2
    y_n = wr[:, CHUNK:] + _b1(_bmm, a_rb, _pair_diag(u)) + akv[:, CHUNK:]
    upd = _b1(_bmm_tn, jnp.concatenate([u, v_n], axis=1), jnp.concatenate([bdec_n, kdec_n], axis=1))
    state[...] = s0 * elast_n + jnp.where(same_head, upd, 0.0)

    yield
    y = _from_chains(y_n, n_pairs).reshape(rows, D_B)
    mu = _bdot(y, head_avg)
    d = y - mu
    var = _bdot(d * d, head_avg)
    yn = d * lax.rsqrt(var + RWKV_LNX_EPS) * lg_ref[...] + lb_ref[...]
    o_ref[...] = ((yn + bonus) * gate_all).reshape(bsz, CHUNK, D_B).astype(o_ref.dtype)


def _ret_log_gamma(h):
    return math.log(1.0 - 2.0 ** (-5.0 - h))


def _per_head(index, width, values):
    out = jnp.full(index.shape, values[-1], F32)
    for h in range(len(values) - 2, -1, -1):
        out = jnp.where(index // width == h, values[h], out)
    return out


def _head_diag(x3, width):
    x3 = x3.astype(BF16)
    heads = x3.shape[2] // width
    lane_head = lax.broadcasted_iota(jnp.int32, (1, 1, x3.shape[2]), 2) // width
    zero = jnp.zeros((), BF16)
    return jnp.concatenate([jnp.where(lane_head == h, x3, zero) for h in range(heads)], axis=1)


def _ret_body(pc_ref, cos_ref, sin_ref, ng_ref, nb_ref, o_ref, state):
    bsz = pc_ref.shape[0]
    rows = bsz * CHUNK
    qk_w = RET_HEADS * RET_DK
    lane = lax.broadcasted_iota(jnp.int32, (bsz, CHUNK, qk_w), 2)
    first_half = (lane % RET_DK) < (RET_DK // 2)

    def rope(t):
        partner = jnp.where(first_half, pltpu.roll(t, qk_w - RET_DK // 2, 2), pltpu.roll(t, RET_DK // 2, 2))
        return t * cos_ref[...] + partner * sin_ref[...]

    log_gamma = [_ret_log_gamma(h) for h in range(RET_HEADS)]
    t_idx = lax.broadcasted_iota(jnp.int32, (CHUNK, D_C), 0)
    lane_v = lax.broadcasted_iota(jnp.int32, (CHUNK, D_C), 1)
    delta = (t_idx - lane_v % CHUNK).astype(F32)
    d_intra = jnp.where(delta >= 0, jnp.exp(_per_head(lane_v, CHUNK, log_gamma) * jnp.maximum(delta, 0.0)), 0.0)
    t_qk = lax.broadcasted_iota(jnp.int32, (CHUNK, qk_w), 0).astype(F32)
    lg_qk = _per_head(lax.broadcasted_iota(jnp.int32, (CHUNK, qk_w), 1), RET_DK, log_gamma)
    q_scale = jnp.exp(lg_qk * (t_qk + 1.0))
    k_scale = jnp.exp(lg_qk * (CHUNK - 1.0 - t_qk))
    s_row = lax.broadcasted_iota(jnp.int32, (qk_w, D_C), 0)
    s_lane = lax.broadcasted_iota(jnp.int32, (qk_w, D_C), 1)
    same_head = s_row // RET_DK == s_lane // RET_DV
    s_scale = _per_head(s_row, RET_DK, [math.exp(lg * CHUNK) for lg in log_gamma])
    head_avg = _head_group_matrix(D_C, RET_DV, 1.0 / RET_DV)

    yield
    q = rope(pc_ref[:, :, 0:qk_w])
    k = rope(pc_ref[:, :, qk_w:2 * qk_w]) * RET_DK ** -0.5
    v = pc_ref[:, :, 2 * qk_w:2 * qk_w + D_C]
    scores = _bmm_nt(q.astype(BF16), _head_diag(k, RET_DK))
    intra = _bmm((scores * d_intra).astype(BF16), _head_diag(v, RET_DV))
    s0 = state[...]
    y3 = intra + _b1(_bmm, q * q_scale, s0)
    state[...] = s0 * s_scale + jnp.where(same_head, _b1(_bmm_tn, k * k_scale, v), 0.0)
    yield
    y = y3.reshape(rows, D_C)
    mu = _bdot(y, head_avg)
    d = y - mu
    var = _bdot(d * d, head_avg)
    yn = d * lax.rsqrt(var + LN_EPS) * ng_ref[...] + nb_ref[...]
    gate = pc_ref[:, :, 2 * qk_w + D_C:].reshape(rows, D_C)
    o_ref[...] = (yn * _silu(gate)).reshape(bsz, CHUNK, D_C).astype(o_ref.dtype)


N_GDN_PARAMS, N_RWKV_PARAMS, N_RET_PARAMS = 4, 11, 4


def _mixers_kernel(*refs):
    pa_ref, pb_ref, pc_ref = refs[:3]
    params = refs[3:3 + N_GDN_PARAMS + N_RWKV_PARAMS + N_RET_PARAMS]
    gdn_p, rwkv_p = params[:N_GDN_PARAMS], params[N_GDN_PARAMS:N_GDN_PARAMS + N_RWKV_PARAMS]
    ret_p = params[N_GDN_PARAMS + N_RWKV_PARAMS:]
    o_ref, cbuf, gdn_state, sbuf, rwkv_state, ret_state = refs[3 + len(params):]
    oa_ref = o_ref.at[:, :, 0:D_A]
    ob_ref = o_ref.at[:, :, D_A:D_A + D_B]
    oc_ref = o_ref.at[:, :, D_A + D_B:]

    @pl.when(pl.program_id(0) == 0)
    def _():
        cbuf[:, 0:HALO, :] = jnp.zeros((cbuf.shape[0], HALO, GDN_QKV), F32)
        sbuf[:, 0:HALO, :] = jnp.zeros((sbuf.shape[0], HALO, D_B_IN), F32)
        gdn_state[...] = jnp.zeros(gdn_state.shape, F32)
        rwkv_state[...] = jnp.zeros(rwkv_state.shape, F32)
        ret_state[...] = jnp.zeros(ret_state.shape, F32)

    bodies = [_gdn_body(pa_ref, *gdn_p, oa_ref, cbuf, gdn_state),
              _rwkv_body(pb_ref, *rwkv_p, ob_ref, sbuf, rwkv_state),
              _ret_body(pc_ref, *ret_p, oc_ref, ret_state)]
    prow = lax.broadcasted_iota(jnp.int32, (CHUNK, 2 * CHUNK), 0)
    pcol = lax.broadcasted_iota(jnp.int32, (CHUNK, 2 * CHUNK), 1) % CHUNK
    eye2 = (prow == pcol).astype(F32)
    inbox = {body: None for body in bodies}
    waiting = {}
    while bodies:
        for body in list(bodies):
            if body in waiting:
                continue
            try:
                request = body.send(inbox[body])
            except StopIteration:
                bodies.remove(body)
                continue
            inbox[body] = None
            if request is not None:
                waiting[body] = request
        if len(waiting) == 2 or (waiting and all(body in waiting for body in bodies)):
            owners = list(waiting)
            inv = _pair_series(jnp.concatenate([waiting[o] for o in owners], axis=0), eye2)
            start = 0
            for o in owners:
                n = waiting[o].shape[0]
                inbox[o] = inv[start:start + n]
                start += n
            waiting.clear()


def _mixers(pa, pb, pc, gdn_params, rwkv_params, ret_params):
    bsz, seq, _ = pa.shape
    n_chunks = seq // CHUNK
    chunked = lambda width: pl.BlockSpec((bsz, CHUNK, width), lambda c: (0, (c + n_chunks - 1) % n_chunks, 0))
    rope_spec = pl.BlockSpec((CHUNK, RET_HEADS * RET_DK), lambda c: (c, 0))
    params = tuple(gdn_params) + tuple(rwkv_params) + tuple(ret_params)
    param_specs = [_const_spec(p.shape) for p in gdn_params + rwkv_params] + \
                  [rope_spec, rope_spec] + [_const_spec(p.shape) for p in ret_params[2:]]
    return pl.pallas_call(
        _mixers_kernel,
        grid=(n_chunks,),
        in_specs=[chunked(D_A_COLS), chunked(D_B_IN), chunked(D_C_IN)] + param_specs,
        out_specs=chunked(D_A + D_B + D_C),
        out_shape=jax.ShapeDtypeStruct((bsz, seq, D_A + D_B + D_C), BF16),
        scratch_shapes=[pltpu.VMEM((bsz, HALO + CHUNK, GDN_QKV), F32),
                        pltpu.VMEM((GDN_HEADS * bsz, GDN_DK, GDN_DV), F32),
                        pltpu.VMEM((bsz, HALO + CHUNK, D_B_IN), F32),
                        pltpu.VMEM((RWKV_HEADS // 2 * bsz, 2 * RWKV_HEAD, 2 * RWKV_HEAD), F32),
                        pltpu.VMEM((bsz, RET_HEADS * RET_DK, D_C), F32)],
        compiler_params=pltpu.CompilerParams(dimension_semantics=("arbitrary",),
                                             vmem_limit_bytes=VMEM_LIMIT_BYTES),
        name="mixers",
    )(pa, pb, pc, *params)


def _rope_tables(seq):
    half = RET_DK // 2
    inv_freq = 1.0 / (ROPE_BASE ** jnp.linspace(0.0, 1.0, half, dtype=F32))
    ang = jnp.arange(seq, dtype=F32)[:, None] * inv_freq
    cos, sin = jnp.cos(ang), jnp.sin(ang)
    cos_t = jnp.tile(jnp.concatenate([cos, cos], -1), (1, RET_HEADS))
    sin_t = jnp.tile(jnp.concatenate([-sin, sin], -1), (1, RET_HEADS))
    return cos_t, sin_t


def _row(v):
    return v.reshape(1, -1)


def _pad_gate_vec(v):
    return jnp.zeros((1, GATE_PAD), F32).at[0, GDN_HEADS:2 * GDN_HEADS].set(v)


def _mixer_layer(h3, w_in, gdn, rwkv, ret, rope):
    wa = jnp.concatenate([w_in[:, :D_A_IN], jnp.zeros((D_MODEL, D_A_COLS - D_A_IN), w_in.dtype)], axis=1)
    wb = w_in[:, D_A_IN:D_A_IN + D_B_IN]
    wc = w_in[:, D_A_IN + D_B_IN:]
    pa, pb, pc = _in_proj(h3, wa.astype(BF16), wb.astype(BF16), wc.astype(BF16))
    conv_w, a_log, dt_bias, norm_w = gdn
    return _mixers(
        pa, pb, pc,
        [conv_w, _pad_gate_vec(a_log), _pad_gate_vec(dt_bias), _row(norm_w)],
        [_row(p) if p.ndim == 1 else p for p in rwkv],
        [rope[0], rope[1], _row(ret[0]), _row(ret[1])])


def kernel(x, meta_tokens, ln_g, ln_b, w_ff1_in, w_ff1_out, w_ff2_in, w_ff2_out, w_in, w_out,
           gdn_conv_w, gdn_a_log, gdn_dt_bias, gdn_norm_w, rwkv_mu, rwkv_w0, rwkv_w_up, rwkv_a0,
           rwkv_a_up, rwkv_g_up, rwkv_k_k, rwkv_k_a, rwkv_r_k, rwkv_lnx_g, rwkv_lnx_b,
           ret_norm_g, ret_norm_b):
    bsz, n_real, d = x.shape
    seq = n_real + CHUNK
    depth = ln_g.shape[0]
    rope = _rope_tables(seq)
    meta_chunk = jnp.concatenate([jnp.zeros((META_PAD, d), x.dtype), meta_tokens.astype(x.dtype)], axis=0)
    ff1_in, ff1_out, ff2_in, ff2_out, w_mix = (w.astype(BF16) for w in (w_ff1_in, w_ff1_out, w_ff2_in, w_ff2_out, w_out))
    h = None
    for l in range(depth):
        ffn1 = (ff1_in, ff1_out, _row(ln_g[l, 0]), _row(ln_b[l, 0]), l)
        if l == 0:
            h = _first_ffn_ln(x, meta_chunk, *ffn1)
        else:
            h = _ffn_ln(h, *ffn1)
        y = _mixer_layer(
            h, w_in[l],
            (gdn_conv_w[l], gdn_a_log[l], gdn_dt_bias[l], gdn_norm_w[l]),
            (rwkv_mu[l], rwkv_w0[l], rwkv_w_up[l], rwkv_a0[l], rwkv_a_up[l], rwkv_g_up[l],
             rwkv_k_k[l], rwkv_k_a[l], rwkv_r_k[l].reshape(-1), rwkv_lnx_g[l], rwkv_lnx_b[l]),
            (ret_norm_g[l], ret_norm_b[l]), rope)
        h = _mix_ffn_ln(h, y, w_mix, _row(ln_g[l, 1]), _row(ln_b[l, 1]),
                        ff2_in, ff2_out, _row(ln_g[l, 2]), _row(ln_b[l, 2]), l,
                        out_seq=n_real if l == depth - 1 else None)
    return h
```
